```python
import math
import jax
import jax.numpy as jnp
from jax import lax
import numpy as np

D_MODEL = 4096
BATCH = 1
SEQ = 16384
DEPTH = 2
DEC_BATCH = 32
DEC_SEQ = 64
PAST_LEN = 2048

CHUNK = 64
D_MIX = D_MODEL
D_A = D_MIX // 2
D_B = D_MIX // 4
D_C = D_MIX - D_A - D_B
H_A = 8
DH_A = D_A // H_A
S5_GROUP = 16
G_B = D_B // S5_GROUP
P_B = 64
DT_MIN = 0.001
DT_MAX = 0.1
CONV_W = 31
D_FF = ((8 * D_MODEL // 3 + 255) // 256) * 256
D_PLE = 256
EPS = 1e-6
SPLIT_SIZES = (D_A, D_A, D_A, D_A, H_A, H_A, D_B, D_C, D_C)
D_IN = 4 * D_A + 2 * H_A + D_B + 2 * D_C

kernel_name = 'hybrid_stream_mlstm_s5_conformer_step'


def split_points():
    return tuple(int(s) for s in np.cumsum(SPLIT_SIZES)[:-1])


def rmsnorm(x, g):
    xf = x.astype(jnp.float32)
    y = xf * lax.rsqrt(jnp.mean(xf * xf, axis=-1, keepdims=True) + EPS)
    return (y * g.astype(jnp.float32)).astype(x.dtype)


def mlstm(q, k, v, i_pre, log_f, c0, n0, m0):
    B_, L, H, DH = q.shape
    blk = min(CHUNK, L)
    nc = L // blk
    f32 = jnp.float32

    def chunks(t):
        t = t.astype(f32).reshape((B_, nc, blk) + t.shape[2:])
        return jnp.swapaxes(jnp.moveaxis(t, 1, 0), 2, 3)

    causal = jnp.tril(jnp.ones((blk, blk), dtype=bool))

    def step(carry, xs):
        c, n, m = carry
        qc, kc, vc, ic, lfc = xs
        b = jnp.cumsum(lfc, axis=-1)
        d = b[..., :, None] - b[..., None, :] + ic[..., None, :]
        d = jnp.where(causal, d, -jnp.inf)
        inter = b + m[..., None]
        mt = jnp.maximum(inter, jnp.max(d, axis=-1))
        s = jnp.einsum('bhtd,bhsd->bhts', qc, kc) * jnp.exp(d - mt[..., None])
        sc_t = jnp.exp(inter - mt)
        num = sc_t[..., None] * jnp.einsum('bhtd,bhde->bhte', qc, c) + jnp.einsum('bhts,bhse->bhte', s, vc)
        den = sc_t * jnp.einsum('bhtd,bhd->bht', qc, n) + jnp.sum(s, axis=-1)
        h = num / jnp.maximum(jnp.abs(den), jnp.exp(-mt))[..., None]
        b_last = b[..., -1]
        m_new = mt[..., -1]
        wl = jnp.exp(b_last[..., None] - b + ic - m_new[..., None])
        sc = jnp.exp(b_last + m - m_new)
        c_new = sc[..., None, None] * c + jnp.einsum('bhs,bhsd,bhse->bhde', wl, kc, vc)
        n_new = sc[..., None] * n + jnp.einsum('bhs,bhsd->bhd', wl, kc)
        return (c_new, n_new, m_new), h

    xs = (chunks(q), chunks(k), chunks(v), chunks(i_pre), chunks(log_f))
    (c1, n1, m1), h = lax.scan(step, (c0.astype(f32), n0.astype(f32), m0.astype(f32)), xs)
    h = jnp.swapaxes(jnp.moveaxis(h, 0, 1), 2, 3).reshape(B_, L, H, DH)
    return h, c1, n1, m1


def s5_ssm(u, h0_re, h0_im, w):
    f32 = jnp.float32
    B_, L, _ = u.shape
    uf = u.astype(f32)
    ug = uf.reshape(B_, L, G_B, S5_GROUP)
    dt = jnp.exp(w['s5_log_dt'].astype(f32))[:, None]
    lr = w['s5_lam_re'].astype(f32)
    li = w['s5_lam_im'].astype(f32)
    mag = jnp.exp(lr * dt)
    ab_re = mag * jnp.cos(li * dt)
    ab_im = mag * jnp.sin(li * dt)
    den = lr * lr + li * li
    nr = ab_re - 1.0
    coef_re = (nr * lr + ab_im * li) / den
    coef_im = (ab_im * lr - nr * li) / den
    br = w['s5_b_re'].astype(f32)
    bi = w['s5_b_im'].astype(f32)
    bb_re = coef_re[..., None] * br - coef_im[..., None] * bi
    bb_im = coef_re[..., None] * bi + coef_im[..., None] * br
    bu_re = jnp.einsum('gpj,blgj->blgp', bb_re, ug)
    bu_im = jnp.einsum('gpj,blgj->blgp', bb_im, ug)
    h0r = h0_re.astype(f32)
    h0i = h0_im.astype(f32)
    bu_re = bu_re.at[:, 0].add(ab_re * h0r - ab_im * h0i)
    bu_im = bu_im.at[:, 0].add(ab_re * h0i + ab_im * h0r)
    a_re = jnp.broadcast_to(ab_re, bu_re.shape)
    a_im = jnp.broadcast_to(ab_im, bu_im.shape)

    def combine(e1, e2):
        a1r, a1i, b1r, b1i = e1
        a2r, a2i, b2r, b2i = e2
        return (a1r * a2r - a1i * a2i, a1r * a2i + a1i * a2r,
                a2r * b1r - a2i * b1i + b2r, a2r * b1i + a2i * b1r + b2i)

    _, _, hr, hi = lax.associative_scan(combine, (a_re, a_im, bu_re, bu_im), axis=1)
    y = (jnp.einsum('gjp,blgp->blgj', w['s5_c_re'].astype(f32), hr)
         - jnp.einsum('gjp,blgp->blgj', w['s5_c_im'].astype(f32), hi))
    y = y.reshape(B_, L, D_B) + w['s5_d'].astype(f32) * uf
    zg = jax.nn.gelu(y)
    out = zg * jax.nn.sigmoid(zg @ w['s5_w_glu'].astype(f32))
    return out.astype(u.dtype), hr[:, -1], hi[:, -1]


def conformer_conv(val, gate, conv_prev, w):
    f32 = jnp.float32
    u = val * jax.nn.sigmoid(gate)
    xp = jnp.concatenate([conv_prev.astype(u.dtype), u], axis=1)
    y = lax.conv_general_dilated(xp, w['conv_w'][:, None, :].astype(u.dtype), (1,), 'VALID',
                                 dimension_numbers=('NWC', 'WIO', 'NWC'),
                                 feature_group_count=D_C)
    y = y.astype(f32) + w['conv_b'].astype(f32)
    mu = jnp.mean(y, axis=-1, keepdims=True)
    var = jnp.mean(jnp.square(y - mu), axis=-1, keepdims=True)
    y = (y - mu) * lax.rsqrt(var + EPS) * w['conv_ln_g'].astype(f32) + w['conv_ln_b'].astype(f32)
    return jax.nn.silu(y).astype(val.dtype), xp[:, -(CONV_W - 1):]


def hybrid_layer(x, pe, c0, n0, m0, s_re0, s_im0, conv0, w):
    f32 = jnp.float32
    B_, L, _ = x.shape
    h = rmsnorm(x, w['g_pre_mix'])
    z = h @ w['w_in']
    q, k, v, o, ig, fg, u_b, c_val, c_gate = jnp.split(z, split_points(), axis=-1)
    i_pre = ig.astype(f32) + w['b_igate'].astype(f32)
    log_f = jax.nn.log_sigmoid(fg.astype(f32) + w['b_fgate'].astype(f32))
    h_a, c1, n1, m1 = mlstm(q.reshape(B_, L, H_A, DH_A),
                            k.reshape(B_, L, H_A, DH_A) * (DH_A ** -0.5),
                            v.reshape(B_, L, H_A, DH_A), i_pre, log_f, c0, n0, m0)
    h_a = rmsnorm(h_a, w['g_mh'].reshape(H_A, DH_A)).reshape(B_, L, D_A)
    y_a = h_a.astype(x.dtype) * jax.nn.sigmoid(o)
    y_b, s_re1, s_im1 = s5_ssm(u_b, s_re0, s_im0, w)
    y_c, conv1 = conformer_conv(c_val, c_gate, conv0, w)
    mix = jnp.concatenate([y_a, y_b, y_c], axis=-1) @ w['w_out']
    x = x + rmsnorm(mix, w['g_post_mix'])
    h = rmsnorm(x, w['g_pre_ffn'])
    f = (jax.nn.silu(h @ w['w_ffn_gate']) * (h @ w['w_ffn_up'])) @ w['w_ffn_down']
    x = x + rmsnorm(f, w['g_post_ffn'])
    e = (pe @ w['w_ple']) * jax.nn.sigmoid(x @ w['w_ple_gate'])
    x = x + rmsnorm(e, w['g_post_ple'])
    return x, c1, n1, m1, s_re1, s_im1, conv1


def setup_inputs(seed: int = 0) -> dict:
    key = jax.random.key(seed)
    keys = iter(jax.random.split(key, 64))
    f32 = jnp.float32

    def nrm(shape, scale):
        return jax.random.normal(next(keys), shape, f32) * scale

    def gain(shape):
        return 1.0 + nrm(shape, 0.05)

    return {
        'x_prompt': nrm((BATCH, SEQ, D_MODEL), 1.0),
        'x_sample': nrm((DEC_BATCH, DEC_SEQ, D_MODEL), 1.0),
        'p_prompt': nrm((DEPTH, BATCH, SEQ, D_PLE), 1.0),
        'p_sample': nrm((DEPTH, DEC_BATCH, DEC_SEQ, D_PLE), 1.0),
        'state_mlstm_c': nrm((DEPTH, DEC_BATCH, H_A, DH_A, DH_A), 0.05),
        'state_mlstm_n': nrm((DEPTH, DEC_BATCH, H_A, DH_A), 0.1),
        'state_mlstm_m': nrm((DEPTH, DEC_BATCH, H_A), 1.0),
        'state_s5_re': nrm((DEPTH, DEC_BATCH, G_B, P_B), 0.1),
        'state_s5_im': nrm((DEPTH, DEC_BATCH, G_B, P_B), 0.1),
        'cache_conv': nrm((DEPTH, DEC_BATCH, CONV_W - 1, D_C), 0.5),
        'g_pre_mix': gain((DEPTH, D_MODEL)),
        'w_in': nrm((DEPTH, D_MODEL, D_IN), D_MODEL ** -0.5),
        'b_igate': nrm((DEPTH, H_A), 0.1),
        'b_fgate': jnp.linspace(3.0, 6.0, H_A, dtype=f32)[None, :] + nrm((DEPTH, H_A), 0.1),
        'g_mh': gain((DEPTH, D_A)),
        's5_lam_re': -0.5 * jnp.exp(nrm((DEPTH, G_B, P_B), 0.05)),
        's5_lam_im': jnp.pi * jnp.broadcast_to(jnp.arange(P_B, dtype=f32), (DEPTH, G_B, P_B)) + nrm((DEPTH, G_B, P_B), 0.01),
        's5_log_dt': jax.random.uniform(next(keys), (DEPTH, G_B), f32, math.log(DT_MIN), math.log(DT_MAX)),
        's5_b_re': nrm((DEPTH, G_B, P_B, S5_GROUP), (2 * S5_GROUP) ** -0.5),
        's5_b_im': nrm((DEPTH, G_B, P_B, S5_GROUP), (2 * S5_GROUP) ** -0.5),
        's5_c_re': nrm((DEPTH, G_B, S5_GROUP, P_B), 0.5),
        's5_c_im': nrm((DEPTH, G_B, S5_GROUP, P_B), 0.5),
        's5_d': nrm((DEPTH, D_B), 0.5),
        's5_w_glu': nrm((DEPTH, D_B, D_B), D_B ** -0.5),
        'conv_w': nrm((DEPTH, CONV_W, D_C), CONV_W ** -0.5),
        'conv_b': nrm((DEPTH, D_C), 0.02),
        'conv_ln_g': gain((DEPTH, D_C)),
        'conv_ln_b': nrm((DEPTH, D_C), 0.02),
        'w_out': nrm((DEPTH, D_MIX, D_MODEL), D_MIX ** -0.5),
        'g_post_mix': gain((DEPTH, D_MODEL)),
        'g_pre_ffn': gain((DEPTH, D_MODEL)),
        'w_ffn_gate': nrm((DEPTH, D_MODEL, D_FF), D_MODEL ** -0.5),
        'w_ffn_up': nrm((DEPTH, D_MODEL, D_FF), D_MODEL ** -0.5),
        'w_ffn_down': nrm((DEPTH, D_FF, D_MODEL), D_FF ** -0.5),
        'g_post_ffn': gain((DEPTH, D_MODEL)),
        'w_ple': nrm((DEPTH, D_PLE, D_MODEL), D_PLE ** -0.5),
        'w_ple_gate': nrm((DEPTH, D_MODEL, D_MODEL), D_MODEL ** -0.5),
        'g_post_ple': gain((DEPTH, D_MODEL)),
    }


def reference(x_prompt, x_sample, p_prompt, p_sample,
              state_mlstm_c, state_mlstm_n, state_mlstm_m, state_s5_re, state_s5_im, cache_conv,
              g_pre_mix, w_in, b_igate, b_fgate, g_mh,
              s5_lam_re, s5_lam_im, s5_log_dt, s5_b_re, s5_b_im, s5_c_re, s5_c_im, s5_d, s5_w_glu,
              conv_w, conv_b, conv_ln_g, conv_ln_b, w_out, g_post_mix,
              g_pre_ffn, w_ffn_gate, w_ffn_up, w_ffn_down, g_post_ffn,
              w_ple, w_ple_gate, g_post_ple):
    f32 = jnp.float32
    b0 = x_prompt.shape[0]
    zc = jnp.zeros((b0, H_A, DH_A, DH_A), f32)
    zn = jnp.zeros((b0, H_A, DH_A), f32)
    zm = jnp.zeros((b0, H_A), f32)
    zs = jnp.zeros((b0, G_B, P_B), f32)
    zconv = jnp.zeros((b0, CONV_W - 1, D_C), x_prompt.dtype)
    yp = x_prompt
    ys = x_sample
    pc, pn, pm, pre, pim, pconv = [], [], [], [], [], []
    sc, sn, sm, sre, sim, sconv = [], [], [], [], [], []
    for i in range(DEPTH):
        w = dict(g_pre_mix=g_pre_mix[i], w_in=w_in[i], b_igate=b_igate[i], b_fgate=b_fgate[i],
                 g_mh=g_mh[i], s5_lam_re=s5_lam_re[i], s5_lam_im=s5_lam_im[i],
                 s5_log_dt=s5_log_dt[i], s5_b_re=s5_b_re[i], s5_b_im=s5_b_im[i],
                 s5_c_re=s5_c_re[i], s5_c_im=s5_c_im[i], s5_d=s5_d[i], s5_w_glu=s5_w_glu[i],
                 conv_w=conv_w[i], conv_b=conv_b[i], conv_ln_g=conv_ln_g[i], conv_ln_b=conv_ln_b[i],
                 w_out=w_out[i], g_post_mix=g_post_mix[i], g_pre_ffn=g_pre_ffn[i],
                 w_ffn_gate=w_ffn_gate[i], w_ffn_up=w_ffn_up[i], w_ffn_down=w_ffn_down[i],
                 g_post_ffn=g_post_ffn[i], w_ple=w_ple[i], w_ple_gate=w_ple_gate[i],
                 g_post_ple=g_post_ple[i])
        yp, c1, n1, m1, r1, im1, cv1 = hybrid_layer(yp, p_prompt[i], zc, zn, zm, zs, zs, zconv, w)
        pc.append(c1); pn.append(n1); pm.append(m1); pre.append(r1); pim.append(im1); pconv.append(cv1)
        ys, c2, n2, m2, r2, im2, cv2 = hybrid_layer(ys, p_sample[i], state_mlstm_c[i], state_mlstm_n[i],
                                                    state_mlstm_m[i], state_s5_re[i], state_s5_im[i],
                                                    cache_conv[i], w)
        sc.append(c2); sn.append(n2); sm.append(m2); sre.append(r2); sim.append(im2); sconv.append(cv2)
    return (yp, ys,
            jnp.stack(pc), jnp.stack(pn), jnp.stack(pm), jnp.stack(pre), jnp.stack(pim), jnp.stack(pconv),
            jnp.stack(sc), jnp.stack(sn), jnp.stack(sm), jnp.stack(sre), jnp.stack(sim), jnp.stack(sconv))
```

```python
import functools
import math

import jax
import jax.numpy as jnp
from jax import lax
from jax.experimental import pallas as pl
from jax.experimental.pallas import tpu as pltpu

F32 = jnp.float32
BF16 = jnp.bfloat16
EPS = 1e-6
LANES = 128
VMEM_LIMIT = 56 * 1024 * 1024
GATE_PAD = LANES
CACHE_PAD = 32


def _pick(n, pref):
    t = min(pref, n)
    while n % t:
        t //= 2
    return t


def _params(*sem):
    return pltpu.CompilerParams(dimension_semantics=sem, vmem_limit_bytes=VMEM_LIMIT)


def _rms(x, g):
    return x * lax.rsqrt(jnp.mean(x * x, axis=-1, keepdims=True) + EPS) * g


def _log_sigmoid(x):
    return jnp.minimum(x, 0.0) - jnp.log1p(jnp.exp(-jnp.abs(x)))


def _dot(a, b):
    return jnp.dot(a, b, preferred_element_type=F32)


def _rms_gates_kernel(x_ref, g_ref, wif_ref, bias_ref, h_ref, gate_ref, *, n_heads):
    hb = _rms(x_ref[...], g_ref[...]).astype(BF16)
    h_ref[...] = hb
    gt = _dot(hb, wif_ref[...]) + bias_ref[...]
    lane = lax.broadcasted_iota(jnp.int32, gt.shape, 1)
    gt = jnp.where(lane >= n_heads, _log_sigmoid(gt), gt)
    gate_ref[...] = gt


def rms_gates(x, g, wif, bias, n_heads):
    n, d = x.shape
    tm = _pick(n, 256)
    return pl.pallas_call(
        functools.partial(_rms_gates_kernel, n_heads=n_heads),
        grid=(n // tm,),
        in_specs=[pl.BlockSpec((tm, d), lambda i: (i, 0)),
                  pl.BlockSpec((1, d), lambda i: (0, 0)),
                  pl.BlockSpec((d, GATE_PAD), lambda i: (0, 0)),
                  pl.BlockSpec((1, GATE_PAD), lambda i: (0, 0))],
        out_specs=[pl.BlockSpec((tm, d), lambda i: (i, 0)),
                   pl.BlockSpec((tm, GATE_PAD), lambda i: (i, 0))],
        out_shape=[jax.ShapeDtypeStruct((n, d), BF16),
                   jax.ShapeDtypeStruct((n, GATE_PAD), F32)],
        compiler_params=_params("parallel"),
    )(x, g, wif, bias)


def _mm_kernel(a_ref, b_ref, o_ref):
    o_ref[...] = _dot(a_ref[...], b_ref[...]).astype(o_ref.dtype)


def mm(a, b, out_dtype):
    m, k = a.shape
    n = b.shape[1]
    tm, tn = _pick(m, 1024), _pick(n, 1024)
    return pl.pallas_call(
        _mm_kernel,
        grid=(m // tm, n // tn),
        in_specs=[pl.BlockSpec((tm, k), lambda i, j: (i, 0)),
                  pl.BlockSpec((k, tn), lambda i, j: (0, j))],
        out_specs=pl.BlockSpec((tm, tn), lambda i, j: (i, j)),
        out_shape=jax.ShapeDtypeStruct((m, n), out_dtype),
        compiler_params=_params("parallel", "arbitrary"),
    )(a, b)


def _mm_swiglu_kernel(a_ref, bg_ref, bu_ref, o_ref):
    a = a_ref[...]
    gate = _dot(a, bg_ref[...])
    up = _dot(a, bu_ref[...])
    o_ref[...] = (gate * jax.nn.sigmoid(gate) * up).astype(o_ref.dtype)


def mm_swiglu(a, bg, bu):
    m, k = a.shape
    n = bg.shape[1]
    tm, tn = _pick(m, 1024), _pick(n, 512)
    return pl.pallas_call(
        _mm_swiglu_kernel,
        grid=(m // tm, n // tn),
        in_specs=[pl.BlockSpec((tm, k), lambda i, j: (i, 0)),
                  pl.BlockSpec((k, tn), lambda i, j: (0, j)),
                  pl.BlockSpec((k, tn), lambda i, j: (0, j))],
        out_specs=pl.BlockSpec((tm, tn), lambda i, j: (i, j)),
        out_shape=jax.ShapeDtypeStruct((m, n), BF16),
        compiler_params=_params("parallel", "arbitrary"),
    )(a, bg, bu)


EPI_ROWS = 64


def _mm_res_kernel(*refs, nk, ple, second):
    if ple:
        a_ref, b_ref, x_ref, g_ref, g2_ref, pe_ref, wple_ref, xo_ref, h_ref = refs
    else:
        a_ref, b_ref, x_ref, g_ref, g2_ref, xo_ref, h_ref = refs
    k = pl.program_id(1)
    part = _dot(a_ref[...], b_ref[...])

    @pl.when(k == 0)
    def _():
        xo_ref[...] = part

    @pl.when(k > 0)
    def _():
        xo_ref[...] += part

    @pl.when(k == nk - 1)
    def _():
        rb = min(EPI_ROWS, xo_ref.shape[0])

        def body(r, carry):
            rows = pl.ds(pl.multiple_of(r * rb, rb), rb)
            f = xo_ref[rows, :]
            if ple:
                f = _dot(pe_ref[rows, :], wple_ref[...]) * jax.nn.sigmoid(f)
            xn = x_ref[rows, :] + _rms(f, g_ref[...])
            xo_ref[rows, :] = xn
            if second == "norm":
                h_ref[rows, :] = _rms(xn, g2_ref[...]).astype(BF16)
            else:
                h_ref[rows, :] = xn.astype(BF16)
            return carry

        lax.fori_loop(0, xo_ref.shape[0] // rb, body, 0)


def mm_res(a, b, x, g, g2, second, pe=None, wple=None, tm_pref=512, tk_pref=512):
    m, kdim = a.shape
    n = b.shape[1]
    tm, tk = _pick(m, tm_pref), _pick(kdim, tk_pref)
    nk = kdim // tk
    ple = pe is not None
    in_specs = [pl.BlockSpec((tm, tk), lambda i, k: (i, k)),
                pl.BlockSpec((tk, n), lambda i, k: (k, 0)),
                pl.BlockSpec((tm, n), lambda i, k: (i, 0), pipeline_mode=pl.Buffered(1)),
                pl.BlockSpec((1, n), lambda i, k: (0, 0)),
                pl.BlockSpec((1, n), lambda i, k: (0, 0))]
    args = [a, b, x, g, g2]
    if ple:
        dp = pe.shape[1]
        in_specs += [pl.BlockSpec((tm, dp), lambda i, k: (i, 0)),
                     pl.BlockSpec((dp, n), lambda i, k: (0, 0))]
        args += [pe, wple]
    return pl.pallas_call(
        functools.partial(_mm_res_kernel, nk=nk, ple=ple, second=second),
        grid=(m // tm, nk),
        in_specs=in_specs,
        out_specs=[pl.BlockSpec((tm, n), lambda i, k: (i, 0)),
                   pl.BlockSpec((tm, n), lambda i, k: (i, 0))],
        out_shape=[jax.ShapeDtypeStruct((m, n), F32),
                   jax.ShapeDtypeStruct((m, n), BF16)],
        compiler_params=_params("parallel", "arbitrary"),
    )(*args)


def _split3(x):
    hi = x.astype(BF16)
    r1 = x - hi.astype(F32)
    mid = r1.astype(BF16)
    lo = (r1 - mid.astype(F32)).astype(BF16)
    return hi, mid, lo


def _mlstm_kernel(zq_ref, zk_ref, zv_ref, zo_ref, gate_ref, c0_ref, n0_ref, m0_ref,
                  gmh_ref, y_ref, c_ref, n_ref, m_ref, *, n_heads, dh, t):
    @pl.when(pl.program_id(1) == 0)
    def _():
        c_ref[...] = c0_ref[...]
        n_ref[...] = n0_ref[...]
        m_ref[...] = m0_ref[...]

    row = lax.broadcasted_iota(jnp.int32, (t, t), 0)
    col = lax.broadcasted_iota(jnp.int32, (t, t), 1)
    causal = row >= col
    tri = causal.astype(BF16)
    tri_t = (col >= row).astype(BF16)
    g_col = gate_ref[...]
    g_row = g_col.T
    cum_col = sum(_dot(tri, p) for p in _split3(g_col))
    cum_row = sum(_dot(p, tri_t) for p in _split3(g_row))
    scale = dh ** -0.5

    for h in range(n_heads):
        sl = slice(h * dh, (h + 1) * dh)
        q = zq_ref[:, sl]
        k = zk_ref[:, sl]
        v = zv_ref[:, sl]
        i_col = g_col[:, h:h + 1]
        i_row = g_row[h:h + 1, :]
        b_col = cum_col[:, n_heads + h:n_heads + h + 1]
        b_row = cum_row[n_heads + h:n_heads + h + 1, :]
        m_prev = m_ref[0, :, h:h + 1]
        c_prev = c_ref[0, h]
        n_prev = n_ref[0, h:h + 1, :]

        d = jnp.where(causal, b_col - b_row + i_row, -jnp.inf)
        inter = b_col + m_prev
        mt = jnp.maximum(inter, jnp.max(d, axis=1, keepdims=True))
        s = lax.dot_general(q, k, (((1,), (1,)), ((), ())), preferred_element_type=F32)
        s = s * scale * jnp.exp(d - mt)
        sc_t = jnp.exp(inter - mt)
        num = sc_t * _dot(q, c_prev.astype(BF16)) + _dot(s.astype(BF16), v)
        qn = jnp.sum(q.astype(F32) * n_prev, axis=1, keepdims=True)
        den = sc_t * qn + jnp.sum(s, axis=1, keepdims=True)
        hh = num / jnp.maximum(jnp.abs(den), jnp.exp(-mt))

        b_last = b_col[t - 1:t, :]
        m_new = mt[t - 1:t, :]
        wl = jnp.exp(b_last - b_col + i_col - m_new)
        sc = jnp.exp(b_last + m_prev - m_new)
        kw = k.astype(F32) * wl
        kv = lax.dot_general(kw.astype(BF16), v, (((0,), (0,)), ((), ())), preferred_element_type=F32)
        c_ref[0, h] = sc * c_prev + kv * scale
        n_ref[0, h:h + 1, :] = sc * n_prev + jnp.sum(kw, axis=0, keepdims=True) * scale
        m_ref[0, :, h:h + 1] = m_new

        hn = _rms(hh, gmh_ref[:, sl])
        y_ref[:, sl] = (hn * jax.nn.sigmoid(zo_ref[:, sl].astype(F32))).astype(BF16)


def mlstm(z, gates, c0, n0, m0, gmh, row_off, bsz, seq, n_heads, dh):
    d_a = n_heads * dh
    t = min(256, seq)
    nc = seq // t
    r0 = row_off // t
    rmap = lambda b, c: r0 + b * nc + c
    zspec = lambda j: pl.BlockSpec((t, d_a), lambda b, c, j=j: (rmap(b, c), j))
    return pl.pallas_call(
        functools.partial(_mlstm_kernel, n_heads=n_heads, dh=dh, t=t),
        grid=(bsz, nc),
        in_specs=[zspec(0), zspec(1), zspec(2), zspec(3),
                  pl.BlockSpec((t, GATE_PAD), lambda b, c: (rmap(b, c), 0)),
                  pl.BlockSpec((1, n_heads, dh, dh), lambda b, c: (b, 0, 0, 0)),
                  pl.BlockSpec((1, n_heads, dh), lambda b, c: (b, 0, 0)),
                  pl.BlockSpec((1, 1, LANES), lambda b, c: (b, 0, 0)),
                  pl.BlockSpec((1, d_a), lambda b, c: (0, 0))],
        out_specs=[pl.BlockSpec((t, d_a), lambda b, c: (b * nc + c, 0)),
                   pl.BlockSpec((1, n_heads, dh, dh), lambda b, c: (b, 0, 0, 0)),
                   pl.BlockSpec((1, n_heads, dh), lambda b, c: (b, 0, 0)),
                   pl.BlockSpec((1, 1, LANES), lambda b, c: (b, 0, 0))],
        out_shape=[jax.ShapeDtypeStruct((bsz * seq, d_a), BF16),
                   jax.ShapeDtypeStruct((bsz, n_heads, dh, dh), F32),
                   jax.ShapeDtypeStruct((bsz, n_heads, dh), F32),
                   jax.ShapeDtypeStruct((bsz, 1, LANES), F32)],
        compiler_params=_params("parallel", "arbitrary"),
    )(z, z, z, z, gates, c0, n0, m0, gmh)


def _s5_disc_kernel(lr_ref, li_ref, ldt_ref, br_ref, bi_ref, abr_ref, abi_ref, bbr_ref, bbi_ref):
    dt = jnp.exp(ldt_ref[...])
    lr = lr_ref[...]
    li = li_ref[...]
    mag = jnp.exp(lr * dt)
    ab_re = mag * jnp.cos(li * dt)
    ab_im = mag * jnp.sin(li * dt)
    den = lr * lr + li * li
    nr = ab_re - 1.0
    coef_re = (nr * lr + ab_im * li) / den
    coef_im = (ab_im * lr - nr * li) / den
    br = br_ref[...]
    bi = bi_ref[...]
    abr_ref[...] = ab_re
    abi_ref[...] = ab_im
    bbr_ref[...] = coef_re * br - coef_im * bi
    bbi_ref[...] = coef_re * bi + coef_im * br


def s5_discretise(lam_re, lam_im, log_dt, b_re, b_im):
    g, p, j = b_re.shape
    rep = lambda a: jnp.repeat(a, j, axis=0)
    br_t = jnp.swapaxes(b_re, 1, 2).reshape(g * j, p)
    bi_t = jnp.swapaxes(b_im, 1, 2).reshape(g * j, p)
    shp = jax.ShapeDtypeStruct((g * j, p), F32)
    abr, abi, bbr, bbi = pl.pallas_call(
        _s5_disc_kernel, out_shape=[shp, shp, shp, shp],
    )(rep(lam_re), rep(lam_im), rep(log_dt[:, None]), br_t, bi_t)
    first = lambda a: a.reshape(g, j, p)[:, 0]
    return first(abr), first(abi), bbr.reshape(g, j, p), bbi.reshape(g, j, p)


def _s5_scan_kernel(*refs, s_rows, tc, nq, gw, slab, emit_y):
    if emit_y:
        u_ref, h0_ref, a_ref, bblk_ref, cblk_ref, d_ref, wglu_ref, y_ref, hout_ref, bu_scr = refs
    else:
        u_ref, h0_ref, a_ref, bblk_ref, hout_ref, bu_scr = refs

    @pl.when(pl.program_id(0) == 0)
    def _():
        hout_ref[...] = h0_ref[...]

    u = u_ref[...]
    kq = u.shape[1] // nq
    for q in range(nq):
        bu_scr[:, q * 2 * gw:(q + 1) * 2 * gw] = _dot(u[:, q * kq:(q + 1) * kq], bblk_ref[q])

    for q in range(nq):
        for lo in range(0, gw, slab):
            re = slice(q * 2 * gw + lo, q * 2 * gw + lo + slab)
            im = slice(q * 2 * gw + gw + lo, q * 2 * gw + gw + lo + slab)
            ar = a_ref[:, re]
            ai = a_ref[:, im]

            def body(step, carry, re=re, im=im, ar=ar, ai=ai):
                hr, hi = carry
                rows = pl.ds(pl.multiple_of(step * s_rows, s_rows), s_rows)
                nr = ar * hr - ai * hi + bu_scr[rows, re]
                ni = ar * hi + ai * hr + bu_scr[rows, im]
                bu_scr[rows, re] = nr
                bu_scr[rows, im] = ni
                return nr, ni

            hr, hi = lax.fori_loop(0, tc, body, (hout_ref[:, re], hout_ref[:, im]))
            hout_ref[:, re] = hr
            hout_ref[:, im] = hi

    if emit_y:
        ys = [_dot(bu_scr[:, q * 2 * gw:(q + 1) * 2 * gw].astype(BF16), cblk_ref[q]) for q in range(nq)]
        y = jnp.concatenate(ys, axis=1) + d_ref[...] * u.astype(F32)
        zg = 0.5 * y * (1.0 + jnp.tanh(math.sqrt(2.0 / math.pi) * (y + 0.044715 * (y * y * y))))
        y_ref[...] = (zg * jax.nn.sigmoid(_dot(zg.astype(BF16), wglu_ref[...]))).astype(BF16)


def s5_scan(u_perm, h0, a_blk, bblk, cblk, d, wglu, s_rows, emit_y):
    rows, d_b = u_perm.shape
    nq, kq, w2 = bblk.shape
    gw = w2 // 2
    steps = rows // s_rows
    tc = _pick(steps, max(1, 256 // s_rows))
    rb = tc * s_rows
    wtot = nq * w2
    const2 = lambda c: (0, 0)
    in_specs = [pl.BlockSpec((rb, d_b), lambda c: (c, 0)),
                pl.BlockSpec((s_rows, wtot), const2),
                pl.BlockSpec((1, wtot), const2),
                pl.BlockSpec((nq, kq, w2), lambda c: (0, 0, 0))]
    args = [u_perm, h0, a_blk, bblk]
    out_specs = [pl.BlockSpec((s_rows, wtot), const2)]
    out_shape = [jax.ShapeDtypeStruct((s_rows, wtot), F32)]
    if emit_y:
        in_specs += [pl.BlockSpec((nq, w2, kq), lambda c: (0, 0, 0)),
                     pl.BlockSpec((1, d_b), const2),
                     pl.BlockSpec((d_b, d_b), const2)]
        args += [cblk, d, wglu]
        out_specs = [pl.BlockSpec((rb, d_b), lambda c: (c, 0))] + out_specs
        out_shape = [jax.ShapeDtypeStruct((rows, d_b), BF16)] + out_shape
    return pl.pallas_call(
        functools.partial(_s5_scan_kernel, s_rows=s_rows, tc=tc, nq=nq, gw=gw,
                          slab=min(512, gw), emit_y=emit_y),
        grid=(steps // tc,),
        in_specs=in_specs, out_specs=out_specs, out_shape=out_shape,
        scratch_shapes=[pltpu.VMEM((rb, wtot), F32)],
        compiler_params=_params("arbitrary"),
    )(*args)


def _s5_chain_kernel(sfin_ref, a_ref, hin_ref, *, seg_len, n_seg, nq, gw):
    for q in range(nq):
        re = slice(q * 2 * gw, q * 2 * gw + gw)
        im = slice(q * 2 * gw + gw, (q + 1) * 2 * gw)
        br, bi = a_ref[:, re], a_ref[:, im]
        pr, pi = jnp.ones_like(br), jnp.zeros_like(br)
        e = seg_len
        while e:
            if e & 1:
                pr, pi = pr * br - pi * bi, pr * bi + pi * br
            br, bi = br * br - bi * bi, 2.0 * br * bi
            e >>= 1
        hr, hi = jnp.zeros_like(pr), jnp.zeros_like(pr)
        for s in range(n_seg):
            hin_ref[s:s + 1, re] = hr
            hin_ref[s:s + 1, im] = hi
            hr, hi = (pr * hr - pi * hi + sfin_ref[s:s + 1, re],
                      pr * hi + pi * hr + sfin_ref[s:s + 1, im])


def s5_chain(sfin, a_blk, seg_len, nq):
    n_seg, wtot = sfin.shape
    return pl.pallas_call(
        functools.partial(_s5_chain_kernel, seg_len=seg_len, n_seg=n_seg, nq=nq, gw=wtot // nq // 2),
        out_shape=jax.ShapeDtypeStruct((n_seg, wtot), F32),
    )(sfin, a_blk)


def _conv_kernel(val_ref, gate_ref, cache_ref, w_ref, b_ref, lng_ref, lnb_ref, y_ref, cout_ref,
                 xp_scr, *, t, width):
    c = pl.program_id(1)

    @pl.when(c == 0)
    def _():
        xp_scr[0:CACHE_PAD, :] = cache_ref[0]

    @pl.when(c > 0)
    def _():
        xp_scr[0:CACHE_PAD, :] = xp_scr[t:t + CACHE_PAD, :]

    xp_scr[CACHE_PAD:CACHE_PAD + t, :] = val_ref[...].astype(F32) * jax.nn.sigmoid(gate_ref[...].astype(F32))
    first = CACHE_PAD - (width - 1)
    acc = w_ref[0:1, :] * xp_scr[first:first + t, :]
    for j in range(1, width):
        acc = acc + w_ref[j:j + 1, :] * xp_scr[first + j:first + j + t, :]
    y = acc + b_ref[...]
    mu = jnp.mean(y, axis=-1, keepdims=True)
    yc = y - mu
    var = jnp.mean(yc * yc, axis=-1, keepdims=True)
    y = yc * lax.rsqrt(var + EPS) * lng_ref[...] + lnb_ref[...]
    y_ref[...] = (y * jax.nn.sigmoid(y)).astype(BF16)
    cout_ref[0] = xp_scr[t:t + CACHE_PAD, :]


def conv_module(z, cache, w, b, lng, lnb, row_off, bsz, seq, col_val, col_gate):
    width, d_c = w.shape
    t = min(256, seq)
    assert t >= CACHE_PAD
    nc = seq // t
    r0 = row_off // t
    rmap = lambda bb, c: r0 + bb * nc + c
    vec = pl.BlockSpec((1, d_c), lambda bb, c: (0, 0))
    return pl.pallas_call(
        functools.partial(_conv_kernel, t=t, width=width),
        grid=(bsz, nc),
        in_specs=[pl.BlockSpec((t, d_c), lambda bb, c: (rmap(bb, c), col_val)),
                  pl.BlockSpec((t, d_c), lambda bb, c: (rmap(bb, c), col_gate)),
                  pl.BlockSpec((1, CACHE_PAD, d_c), lambda bb, c: (bb, 0, 0)),
                  pl.BlockSpec((width, d_c), lambda bb, c: (0, 0)),
                  vec, vec, vec],
        out_specs=[pl.BlockSpec((t, d_c), lambda bb, c: (bb * nc + c, 0)),
                   pl.BlockSpec((1, CACHE_PAD, d_c), lambda bb, c: (bb, 0, 0))],
        out_shape=[jax.ShapeDtypeStruct((bsz * seq, d_c), BF16),
                   jax.ShapeDtypeStruct((bsz, CACHE_PAD, d_c), F32)],
        scratch_shapes=[pltpu.VMEM((CACHE_PAD + t, d_c), F32)],
        compiler_params=_params("parallel", "arbitrary"),
    )(z, z, cache, w, b, lng, lnb)


def _block_diag(w, gq):
    g, r, c = w.shape
    eye = jnp.eye(gq, dtype=w.dtype)
    w = w.reshape(g // gq, gq, r, c)
    return jnp.einsum("qgrc,gh->qgrhc", w, eye).reshape(g // gq, gq * r, gq * c)


def _state_to_lanes(re, im, gq):
    s, g, p = re.shape
    both = jnp.stack([re.reshape(s, g // gq, gq * p), im.reshape(s, g // gq, gq * p)], axis=2)
    return both.reshape(s, -1)


def _lanes_to_state(h, g, p, gq):
    s = h.shape[0]
    both = h.reshape(s, g // gq, 2, gq * p)
    return both[:, :, 0].reshape(s, g, p), both[:, :, 1].reshape(s, g, p)


def _to_step_major(x, n_seq):
    rows, d = x.shape
    return x.reshape(n_seq, rows // n_seq, d).swapaxes(0, 1).reshape(rows, d)


def _from_step_major(x, n_seq):
    rows, d = x.shape
    return x.reshape(rows // n_seq, n_seq, d).swapaxes(0, 1).reshape(rows, d)


PROMPT_SEGMENTS = 32


def kernel(x_prompt, x_sample, p_prompt, p_sample, state_mlstm_c, state_mlstm_n, state_mlstm_m, state_s5_re, state_s5_im, cache_conv, g_pre_mix, w_in, b_igate, b_fgate, g_mh, s5_lam_re, s5_lam_im, s5_log_dt, s5_b_re, s5_b_im, s5_c_re, s5_c_im, s5_d, s5_w_glu, conv_w, conv_b, conv_ln_g, conv_ln_b, w_out, g_post_mix, g_pre_ffn, w_ffn_gate, w_ffn_up, w_ffn_down, g_post_ffn, w_ple, w_ple_gate, g_post_ple):
    depth = w_in.shape[0]
    bp, lp, d_model = x_prompt.shape
    bs, ls, _ = x_sample.shape
    n_heads = b_igate.shape[1]
    d_a = g_mh.shape[1]
    dh = d_a // n_heads
    g_b, p_b, j_b = s5_b_re.shape[1:]
    d_b = g_b * j_b
    width, d_c = conv_w.shape[1:]
    d_ff = w_ffn_gate.shape[2]
    assert bp == 1 and 2 * n_heads <= GATE_PAD and d_b == d_c and d_a % d_b == 0
    gq = 256 // j_b
    nq = g_b // gq
    np_rows, ns_rows = bp * lp, bs * ls
    n_seg = min(PROMPT_SEGMENTS, lp // 8)
    seg_len = lp // n_seg
    col_u = 4 * d_a // d_b
    col_val, col_gate = col_u + 1, col_u + 2
    d_ff_pad = -(-d_ff // 512) * 512

    x = jnp.concatenate([x_prompt.reshape(np_rows, d_model), x_sample.reshape(ns_rows, d_model)], axis=0)
    row = lambda v: v.reshape(1, -1).astype(F32)
    zeros = lambda *s: jnp.zeros(s, F32)
    pad_m = lambda m: jnp.pad(m, ((0, 0), (0, LANES - n_heads)))[:, None, :]
    pad_cache = lambda c: jnp.pad(c, ((0, 0), (CACHE_PAD - (width - 1), 0), (0, 0)))

    outs = {k: [] for k in ("pc", "pn", "pm", "pre", "pim", "pcv", "sc", "sn", "sm", "sre", "sim", "scv")}
    for i in range(depth):
        n_qkvo = 4 * d_a
        w_main = jnp.concatenate([w_in[i][:, :n_qkvo], w_in[i][:, n_qkvo + 2 * n_heads:]], axis=1).astype(BF16)
        w_if = jnp.pad(w_in[i][:, n_qkvo:n_qkvo + 2 * n_heads], ((0, 0), (0, GATE_PAD - 2 * n_heads))).astype(BF16)
        b_if = jnp.pad(jnp.concatenate([b_igate[i], b_fgate[i]]), (0, GATE_PAD - 2 * n_heads)).reshape(1, GATE_PAD)
        w_gate = jnp.pad(w_ffn_gate[i], ((0, 0), (0, d_ff_pad - d_ff))).astype(BF16)
        w_up = jnp.pad(w_ffn_up[i], ((0, 0), (0, d_ff_pad - d_ff))).astype(BF16)
        w_down = jnp.pad(w_ffn_down[i], ((0, d_ff_pad - d_ff), (0, 0))).astype(BF16)

        ab_re, ab_im, bb_re, bb_im = s5_discretise(s5_lam_re[i], s5_lam_im[i], s5_log_dt[i], s5_b_re[i], s5_b_im[i])
        bblk = jnp.concatenate([_block_diag(bb_re, gq), _block_diag(bb_im, gq)], axis=2).astype(BF16)
        c_re_t = jnp.swapaxes(s5_c_re[i], 1, 2)
        c_im_t = jnp.swapaxes(s5_c_im[i], 1, 2)
        cblk = jnp.concatenate([_block_diag(c_re_t, gq), _block_diag(-c_im_t, gq)], axis=1).astype(BF16)
        a_blk = _state_to_lanes(ab_re[None], ab_im[None], gq)
        wglu = s5_w_glu[i].astype(BF16)
        d_row = row(s5_d[i])

        h, gates = rms_gates(x, row(g_pre_mix[i]), w_if, b_if, n_heads)
        z = mm(h, w_main, BF16)

        ya_p, c1, n1, m1 = mlstm(z, gates,zeros(bp, n_heads, dh, dh), zeros(bp, n_heads, dh),
                                 zeros(bp, 1, LANES), row(g_mh[i]), 0, bp, lp, n_heads, dh)
        ya_s, c2, n2, m2 = mlstm(z, gates,state_mlstm_c[i], state_mlstm_n[i], pad_m(state_mlstm_m[i]),
                                 row(g_mh[i]), np_rows, bs, ls, n_heads, dh)

        u_all = z[:, 4 * d_a:4 * d_a + d_b]
        u_p = _to_step_major(u_all[:np_rows], n_seg)
        u_s = _to_step_major(u_all[np_rows:], bs)
        wtot = a_blk.shape[1]
        (sfin,) = s5_scan(u_p, zeros(n_seg, wtot), a_blk, bblk, None, None, None, n_seg, False)
        hin = s5_chain(sfin, a_blk, seg_len, nq)
        yb_p, hfin_p = s5_scan(u_p, hin, a_blk, bblk, cblk, d_row, wglu, n_seg, True)
        yb_s, hfin_s = s5_scan(u_s, _state_to_lanes(state_s5_re[i], state_s5_im[i], gq), a_blk, bblk, cblk,
                               d_row, wglu, bs, True)
        yb_p = _from_step_major(yb_p, n_seg)
        yb_s = _from_step_major(yb_s, bs)
        re_p, im_p = _lanes_to_state(hfin_p[n_seg - 1:], g_b, p_b, gq)
        re_s, im_s = _lanes_to_state(hfin_s, g_b, p_b, gq)

        cw, cb, lg, lb = conv_w[i], row(conv_b[i]), row(conv_ln_g[i]), row(conv_ln_b[i])
        yc_p, cv_p = conv_module(z, zeros(bp, CACHE_PAD, d_c), cw, cb, lg, lb, 0, bp, lp, col_val, col_gate)
        yc_s, cv_s = conv_module(z, pad_cache(cache_conv[i]), cw, cb, lg, lb, np_rows, bs, ls, col_val, col_gate)

        y_mix = jnp.concatenate([jnp.concatenate([ya_p, yb_p, yc_p], axis=1),
                                 jnp.concatenate([ya_s, yb_s, yc_s], axis=1)], axis=0)
        x, h = mm_res(y_mix, w_out[i].astype(BF16), x, row(g_post_mix[i]), row(g_pre_ffn[i]), "norm")

        f = mm_swiglu(h, w_gate, w_up)
        x, xb = mm_res(f, w_down, x, row(g_post_ffn[i]), row(g_post_ffn[i]), "cast")

        pe = jnp.concatenate([p_prompt[i].reshape(np_rows, -1), p_sample[i].reshape(ns_rows, -1)], axis=0).astype(BF16)
        x, _ = mm_res(xb, w_ple_gate[i].astype(BF16), x, row(g_post_ple[i]), row(g_post_ple[i]), "cast",
                      pe=pe, wple=w_ple[i].astype(BF16))

        first = CACHE_PAD - (width - 1)
        outs["pc"].append(c1); outs["pn"].append(n1); outs["pm"].append(m1[:, 0, :n_heads])
        outs["pre"].append(re_p); outs["pim"].append(im_p); outs["pcv"].append(cv_p[:, first:])
        outs["sc"].append(c2); outs["sn"].append(n2); outs["sm"].append(m2[:, 0, :n_heads])
        outs["sre"].append(re_s); outs["sim"].append(im_s); outs["scv"].append(cv_s[:, first:])

    st = lambda k: jnp.stack(outs[k])
    return (x[:np_rows].reshape(bp, lp, d_model), x[np_rows:].reshape(bs, ls, d_model),
            st("pc"), st("pn"), st("pm"), st("pre"), st("pim"), st("pcv"),
            st("sc"), st("sn"), st("sm"), st("sre"), st("sim"), st("scv"))
```

```python
import functools
import math

import jax
import jax.numpy as jnp
from jax import lax
from jax.experimental import pallas as pl
from jax.experimental.pallas import tpu as pltpu

F32 = jnp.float32
BF16 = jnp.bfloat16
EPS = 1e-6
LANES = 128
VMEM_LIMIT = 56 * 1024 * 1024
GATE_PAD = LANES
CACHE_PAD = 32


def _pick(n, pref):
    t = min(pref, n)
    while n % t:
        t //= 2
    return t


def _params(*sem):
    return pltpu.CompilerParams(dimension_semantics=sem, vmem_limit_bytes=VMEM_LIMIT)


def _rms(x, g):
    return x * lax.rsqrt(jnp.mean(x * x, axis=-1, keepdims=True) + EPS) * g


def _log_sigmoid(x):
    return jnp.minimum(x, 0.0) - jnp.log1p(jnp.exp(-jnp.abs(x)))


def _dot(a, b):
    return jnp.dot(a, b, preferred_element_type=F32)


def _gate_act(gt, n_heads):
    lane = lax.broadcasted_iota(jnp.int32, gt.shape, 1)
    return jnp.where(lane >= n_heads, _log_sigmoid(gt), gt)


def _rms_gates_kernel(x_ref, g_ref, wif_ref, bias_ref, h_ref, gate_ref, *, n_heads):
    hb = _rms(x_ref[...], g_ref[...]).astype(BF16)
    h_ref[...] = hb
    gate_ref[...] = _gate_act(_dot(hb, wif_ref[...]) + bias_ref[...], n_heads)


def rms_gates(x, g, wif, bias, n_heads):
    n, d = x.shape
    tm = _pick(n, 256)
    return pl.pallas_call(
        functools.partial(_rms_gates_kernel, n_heads=n_heads),
        grid=(n // tm,),
        in_specs=[pl.BlockSpec((tm, d), lambda i: (i, 0)),
                  pl.BlockSpec((1, d), lambda i: (0, 0)),
                  pl.BlockSpec((d, GATE_PAD), lambda i: (0, 0)),
                  pl.BlockSpec((1, GATE_PAD), lambda i: (0, 0))],
        out_specs=[pl.BlockSpec((tm, d), lambda i: (i, 0)),
                   pl.BlockSpec((tm, GATE_PAD), lambda i: (i, 0))],
        out_shape=[jax.ShapeDtypeStruct((n, d), BF16),
                   jax.ShapeDtypeStruct((n, GATE_PAD), F32)],
        compiler_params=_params("parallel"), name="rms_gates",
    )(x, g, wif, bias)


def _mm_kernel(a_ref, b_ref, o_ref):
    o_ref[...] = _dot(a_ref[...], b_ref[...]).astype(o_ref.dtype)


def mm(a, b, out_dtype):
    m, k = a.shape
    n = b.shape[1]
    tm, tn = _pick(m, 1024), _pick(n, 1024)
    return pl.pallas_call(
        _mm_kernel,
        grid=(m // tm, n // tn),
        in_specs=[pl.BlockSpec((tm, k), lambda i, j: (i, 0)),
                  pl.BlockSpec((k, tn), lambda i, j: (0, j))],
        out_specs=pl.BlockSpec((tm, tn), lambda i, j: (i, j)),
        out_shape=jax.ShapeDtypeStruct((m, n), out_dtype),
        compiler_params=_params("parallel", "arbitrary"), name="mm_in",
    )(a, b)


def _mm_swiglu_kernel(a_ref, bg_ref, bu_ref, o_ref):
    a = a_ref[...]
    gate = _dot(a, bg_ref[...])
    up = _dot(a, bu_ref[...])
    o_ref[...] = (gate * jax.nn.sigmoid(gate) * up).astype(o_ref.dtype)


def mm_swiglu(a, bg, bu):
    m, k = a.shape
    n = bg.shape[1]
    tm, tn = _pick(m, 1024), _pick(n, 512)
    return pl.pallas_call(
        _mm_swiglu_kernel,
        grid=(m // tm, n // tn),
        in_specs=[pl.BlockSpec((tm, k), lambda i, j: (i, 0)),
                  pl.BlockSpec((k, tn), lambda i, j: (0, j)),
                  pl.BlockSpec((k, tn), lambda i, j: (0, j))],
        out_specs=pl.BlockSpec((tm, tn), lambda i, j: (i, j)),
        out_shape=jax.ShapeDtypeStruct((m, n), BF16),
        compiler_params=_params("parallel", "arbitrary"), name="mm_swiglu",
    )(a, bg, bu)


EPI_ROWS = 64


def _mm_res_kernel(*refs, n_a, split, n_steps, tn, ple, second, gates, n_heads):
    refs = list(refs)
    a_refs = [refs.pop(0) for _ in range(n_a)]
    b_ref, x_ref, g_ref = refs.pop(0), refs.pop(0), refs.pop(0)
    g2_ref = refs.pop(0) if second == "norm" else None
    pe_ref, wple_ref = (refs.pop(0), refs.pop(0)) if ple else (None, None)
    wif_ref, bif_ref = (refs.pop(0), refs.pop(0)) if gates else (None, None)
    xo_ref = refs.pop(0)
    h_ref = refs.pop(0) if second else None
    gate_ref = refs.pop(0) if gates else None
    step = pl.program_id(1)

    if split == "n":
        part, k0 = None, 0
        for a_ref in a_refs:
            kw = a_ref.shape[1]
            d = _dot(a_ref[...], b_ref[k0:k0 + kw, :])
            part = d if part is None else part + d
            k0 += kw
        xo_ref[:, pl.ds(pl.multiple_of(step * tn, tn), tn)] = part
    else:
        @pl.when(step == 0)
        def _():
            xo_ref[...] = jnp.zeros_like(xo_ref)

        xo_ref[...] += _dot(a_refs[0][...], b_ref[...])

    @pl.when(step == n_steps - 1)
    def _():
        rb = min(EPI_ROWS, xo_ref.shape[0])

        def body(r, carry):
            rows = pl.ds(pl.multiple_of(r * rb, rb), rb)
            f = xo_ref[rows, :]
            if ple:
                f = _dot(pe_ref[rows, :], wple_ref[...]) * jax.nn.sigmoid(f)
            xn = x_ref[rows, :] + _rms(f, g_ref[...])
            xo_ref[rows, :] = xn
            if second == "norm":
                hb = _rms(xn, g2_ref[...]).astype(BF16)
                h_ref[rows, :] = hb
                if gates:
                    gate_ref[rows, :] = _gate_act(_dot(hb, wif_ref[...]) + bif_ref[...], n_heads)
            elif second == "cast":
                h_ref[rows, :] = xn.astype(BF16)
            return carry

        lax.fori_loop(0, xo_ref.shape[0] // rb, body, 0)


def mm_res(a_parts, b, x, g, *, split, second=None, g2=None, pe=None, wple=None, wif=None, bif=None,
           n_heads=0, tm_pref=512, step_pref=512, name="mm_res"):
    m = x.shape[0]
    kdim, n = b.shape
    tm = _pick(m, tm_pref)
    ple, gates = pe is not None, wif is not None
    once = pl.Buffered(1)
    const = lambda shape: pl.BlockSpec(shape, lambda i, s: (0, 0), pipeline_mode=once)
    if split == "n":
        tn = _pick(n, step_pref)
        n_steps = n // tn
        in_specs = [pl.BlockSpec((tm, a.shape[1]), lambda i, s: (i, 0)) for a in a_parts]
        in_specs.append(pl.BlockSpec((kdim, tn), lambda i, s: (0, s)))
    else:
        (a,) = a_parts
        tn = n
        tk = _pick(kdim, step_pref)
        n_steps = kdim // tk
        in_specs = [pl.BlockSpec((tm, tk), lambda i, s: (i, s)),
                    pl.BlockSpec((tk, n), lambda i, s: (s, 0))]
    in_specs += [pl.BlockSpec((tm, n), lambda i, s: (i, 0), pipeline_mode=once), const((1, n))]
    args = list(a_parts) + [b, x, g]
    if second == "norm":
        in_specs.append(const((1, n)))
        args.append(g2)
    if ple:
        in_specs += [pl.BlockSpec((tm, pe.shape[1]), lambda i, s: (i, 0)), const(wple.shape)]
        args += [pe, wple]
    if gates:
        in_specs += [const(wif.shape), const(bif.shape)]
        args += [wif, bif]
    row_tile = lambda w: pl.BlockSpec((tm, w), lambda i, s: (i, 0))
    out_specs, out_shape = [row_tile(n)], [jax.ShapeDtypeStruct((m, n), F32)]
    if second:
        out_specs.append(row_tile(n))
        out_shape.append(jax.ShapeDtypeStruct((m, n), BF16))
    if gates:
        out_specs.append(row_tile(GATE_PAD))
        out_shape.append(jax.ShapeDtypeStruct((m, GATE_PAD), F32))
    return pl.pallas_call(
        functools.partial(_mm_res_kernel, n_a=len(a_parts), split=split, n_steps=n_steps, tn=tn, ple=ple,
                          second=second, gates=gates, n_heads=n_heads),
        grid=(m // tm, n_steps),
        in_specs=in_specs, out_specs=out_specs, out_shape=out_shape,
        compiler_params=_params("parallel", "arbitrary"), name=name,
    )(*args)


def _split3(x):
    hi = x.astype(BF16)
    r1 = x - hi.astype(F32)
    mid = r1.astype(BF16)
    lo = (r1 - mid.astype(F32)).astype(BF16)
    return hi, mid, lo


def _mlstm_kernel(zq_ref, zk_ref, zv_ref, zo_ref, gate_ref, c0_ref, n0_ref, m0_ref,
                  gmh_ref, *rest, n_heads, dh, t):
    y_ref, c_ref, n_ref, m_ref = rest[-4:]

    @pl.when(pl.program_id(1) == 0)
    def _():
        c_ref[...] = c0_ref[...]
        n_ref[...] = n0_ref[...]
        m_ref[...] = m0_ref[...]

    row = lax.broadcasted_iota(jnp.int32, (t, t), 0)
    col = lax.broadcasted_iota(jnp.int32, (t, t), 1)
    causal = row >= col
    tri = causal.astype(BF16)
    tri_t = (col >= row).astype(BF16)
    g_col = gate_ref[...]
    g_row = g_col.T
    cum_col = sum(_dot(tri, p) for p in _split3(g_col))
    cum_row = sum(_dot(p, tri_t) for p in _split3(g_row))
    scale = dh ** -0.5

    for h in range(n_heads):
        sl = slice(h * dh, (h + 1) * dh)
        q = zq_ref[:, sl]
        k = zk_ref[:, sl]
        v = zv_ref[:, sl]
        i_col = g_col[:, h:h + 1]
        i_row = g_row[h:h + 1, :]
        b_col = cum_col[:, n_heads + h:n_heads + h + 1]
        b_row = cum_row[n_heads + h:n_heads + h + 1, :]
        m_prev = m_ref[0, :, h:h + 1]
        c_prev = c_ref[0, h]
        n_prev = n_ref[0, h:h + 1, :]

        d = jnp.where(causal, b_col - b_row + i_row, -jnp.inf)
        inter = b_col + m_prev
        mt = jnp.maximum(inter, jnp.max(d, axis=1, keepdims=True))
        s = lax.dot_general(q, k, (((1,), (1,)), ((), ())), preferred_element_type=F32)
        s = s * scale * jnp.exp(d - mt)
        sc_t = jnp.exp(inter - mt)
        num = sc_t * _dot(q, c_prev.astype(BF16)) + _dot(s.astype(BF16), v)
        qn = jnp.sum(q.astype(F32) * n_prev, axis=1, keepdims=True)
        den = sc_t * qn + jnp.sum(s, axis=1, keepdims=True)
        hh = num / jnp.maximum(jnp.abs(den), jnp.exp(-mt))

        b_last = b_col[t - 1:t, :]
        m_new = mt[t - 1:t, :]
        wl = jnp.exp(b_last - b_col + i_col - m_new)
        sc = jnp.exp(b_last + m_prev - m_new)
        kw = k.astype(F32) * wl
        kv = lax.dot_general(kw.astype(BF16), v, (((0,), (0,)), ((), ())), preferred_element_type=F32)
        c_ref[0, h] = sc * c_prev + kv * scale
        n_ref[0, h:h + 1, :] = sc * n_prev + jnp.sum(kw, axis=0, keepdims=True) * scale
        m_ref[0, :, h:h + 1] = m_new

        hn = _rms(hh, gmh_ref[:, sl])
        y_ref[:, sl] = (hn * jax.nn.sigmoid(zo_ref[:, sl].astype(F32))).astype(BF16)


def mlstm(z, gates, c0, n0, m0, gmh, y_prev, row_off, bsz, seq, n_heads, dh):
    d_a = n_heads * dh
    n_rows = z.shape[0]
    t = min(256, seq)
    nc = seq // t
    r0 = row_off // t
    rmap = lambda b, c: r0 + b * nc + c
    zspec = lambda j: pl.BlockSpec((t, d_a), lambda b, c, j=j: (rmap(b, c), j))
    alias = [] if y_prev is None else [y_prev]
    return pl.pallas_call(
        functools.partial(_mlstm_kernel, n_heads=n_heads, dh=dh, t=t),
        grid=(bsz, nc),
        input_output_aliases={9: 0} if alias else {},
        in_specs=[zspec(0), zspec(1), zspec(2), zspec(3),
                  pl.BlockSpec((t, GATE_PAD), lambda b, c: (rmap(b, c), 0)),
                  pl.BlockSpec((1, n_heads, dh, dh), lambda b, c: (b, 0, 0, 0)),
                  pl.BlockSpec((1, n_heads, dh), lambda b, c: (b, 0, 0)),
                  pl.BlockSpec((1, 1, LANES), lambda b, c: (b, 0, 0)),
                  pl.BlockSpec((1, d_a), lambda b, c: (0, 0))]
        + [pl.BlockSpec(memory_space=pl.ANY)] * len(alias),
        out_specs=[pl.BlockSpec((t, d_a), lambda b, c: (rmap(b, c), 0)),
                   pl.BlockSpec((1, n_heads, dh, dh), lambda b, c: (b, 0, 0, 0)),
                   pl.BlockSpec((1, n_heads, dh), lambda b, c: (b, 0, 0)),
                   pl.BlockSpec((1, 1, LANES), lambda b, c: (b, 0, 0))],
        out_shape=[jax.ShapeDtypeStruct((n_rows, d_a), BF16),
                   jax.ShapeDtypeStruct((bsz, n_heads, dh, dh), F32),
                   jax.ShapeDtypeStruct((bsz, n_heads, dh), F32),
                   jax.ShapeDtypeStruct((bsz, 1, LANES), F32)],
        compiler_params=_params("parallel", "arbitrary"), name="mlstm",
    )(z, z, z, z, gates, c0, n0, m0, gmh, *alias)


def _s5_disc_kernel(lr_ref, li_ref, ldt_ref, br_ref, bi_ref, abr_ref, abi_ref, bbr_ref, bbi_ref):
    dt = jnp.exp(ldt_ref[...])
    lr = lr_ref[...]
    li = li_ref[...]
    mag = jnp.exp(lr * dt)
    ab_re = mag * jnp.cos(li * dt)
    ab_im = mag * jnp.sin(li * dt)
    den = lr * lr + li * li
    nr = ab_re - 1.0
    coef_re = (nr * lr + ab_im * li) / den
    coef_im = (ab_im * lr - nr * li) / den
    br = br_ref[...]
    bi = bi_ref[...]
    abr_ref[...] = ab_re
    abi_ref[...] = ab_im
    bbr_ref[...] = coef_re * br - coef_im * bi
    bbi_ref[...] = coef_re * bi + coef_im * br


def s5_discretise(lam_re, lam_im, log_dt, b_re, b_im):
    g, p, j = b_re.shape
    rep = lambda a: jnp.repeat(a, j, axis=0)
    br_t = jnp.swapaxes(b_re, 1, 2).reshape(g * j, p)
    bi_t = jnp.swapaxes(b_im, 1, 2).reshape(g * j, p)
    shp = jax.ShapeDtypeStruct((g * j, p), F32)
    abr, abi, bbr, bbi = pl.pallas_call(
        _s5_disc_kernel, out_shape=[shp, shp, shp, shp], name="s5_disc",
    )(rep(lam_re), rep(lam_im), rep(log_dt[:, None]), br_t, bi_t)
    first = lambda a: a.reshape(g, j, p)[:, 0]
    return first(abr), first(abi), bbr.reshape(g, j, p), bbi.reshape(g, j, p)


def _s5_scan_kernel(*refs, s_rows, tc, nq, gw, slab, emit_y):
    if emit_y:
        u_ref, h0_ref, a_ref, bblk_ref, cblk_ref, d_ref, wglu_ref, y_ref, hout_ref, bu_scr = refs
    else:
        u_ref, h0_ref, a_ref, bblk_ref, hout_ref, bu_scr = refs

    @pl.when(pl.program_id(0) == 0)
    def _():
        hout_ref[...] = h0_ref[...]

    u = u_ref[...]
    kq = u.shape[1] // nq
    for q in range(nq):
        bu_scr[:, q * 2 * gw:(q + 1) * 2 * gw] = _dot(u[:, q * kq:(q + 1) * kq], bblk_ref[q])

    for q in range(nq):
        for lo in range(0, gw, slab):
            re = slice(q * 2 * gw + lo, q * 2 * gw + lo + slab)
            im = slice(q * 2 * gw + gw + lo, q * 2 * gw + gw + lo + slab)
            ar = a_ref[:, re]
            ai = a_ref[:, im]

            def body(step, carry, re=re, im=im, ar=ar, ai=ai):
                hr, hi = carry
                rows = pl.ds(pl.multiple_of(step * s_rows, s_rows), s_rows)
                nr = ar * hr - ai * hi + bu_scr[rows, re]
                ni = ar * hi + ai * hr + bu_scr[rows, im]
                bu_scr[rows, re] = nr
                bu_scr[rows, im] = ni
                return nr, ni

            hr, hi = lax.fori_loop(0, tc, body, (hout_ref[:, re], hout_ref[:, im]))
            hout_ref[:, re] = hr
            hout_ref[:, im] = hi

    if emit_y:
        ys = [_dot(bu_scr[:, q * 2 * gw:(q + 1) * 2 * gw].astype(BF16), cblk_ref[q]) for q in range(nq)]
        y = jnp.concatenate(ys, axis=1) + d_ref[...] * u.astype(F32)
        zg = 0.5 * y * (1.0 + jnp.tanh(math.sqrt(2.0 / math.pi) * (y + 0.044715 * (y * y * y))))
        y_ref[...] = (zg * jax.nn.sigmoid(_dot(zg.astype(BF16), wglu_ref[...]))).astype(BF16)


def s5_scan(u_perm, h0, a_blk, bblk, cblk, d, wglu, s_rows, emit_y):
    rows, d_b = u_perm.shape
    nq, kq, w2 = bblk.shape
    gw = w2 // 2
    steps = rows // s_rows
    tc = _pick(steps, max(1, 256 // s_rows))
    rb = tc * s_rows
    wtot = nq * w2
    const2 = lambda c: (0, 0)
    in_specs = [pl.BlockSpec((rb, d_b), lambda c: (c, 0)),
                pl.BlockSpec((s_rows, wtot), const2),
                pl.BlockSpec((1, wtot), const2),
                pl.BlockSpec((nq, kq, w2), lambda c: (0, 0, 0))]
    args = [u_perm, h0, a_blk, bblk]
    out_specs = [pl.BlockSpec((s_rows, wtot), const2)]
    out_shape = [jax.ShapeDtypeStruct((s_rows, wtot), F32)]
    if emit_y:
        in_specs += [pl.BlockSpec((nq, w2, kq), lambda c: (0, 0, 0)),
                     pl.BlockSpec((1, d_b), const2),
                     pl.BlockSpec((d_b, d_b), const2)]
        args += [cblk, d, wglu]
        out_specs = [pl.BlockSpec((rb, d_b), lambda c: (c, 0))] + out_specs
        out_shape = [jax.ShapeDtypeStruct((rows, d_b), BF16)] + out_shape
    return pl.pallas_call(
        functools.partial(_s5_scan_kernel, s_rows=s_rows, tc=tc, nq=nq, gw=gw,
                          slab=min(512, gw), emit_y=emit_y),
        grid=(steps // tc,),
        in_specs=in_specs, out_specs=out_specs, out_shape=out_shape,
        scratch_shapes=[pltpu.VMEM((rb, wtot), F32)],
        compiler_params=_params("arbitrary"), name="s5_scan" if emit_y else "s5_ends",
    )(*args)


def _s5_chain_kernel(sfin_ref, a_ref, hin_ref, *, seg_len, n_seg, nq, gw):
    for q in range(nq):
        re = slice(q * 2 * gw, q * 2 * gw + gw)
        im = slice(q * 2 * gw + gw, (q + 1) * 2 * gw)
        br, bi = a_ref[:, re], a_ref[:, im]
        pr, pi = jnp.ones_like(br), jnp.zeros_like(br)
        e = seg_len
        while e:
            if e & 1:
                pr, pi = pr * br - pi * bi, pr * bi + pi * br
            br, bi = br * br - bi * bi, 2.0 * br * bi
            e >>= 1
        hr, hi = jnp.zeros_like(pr), jnp.zeros_like(pr)
        for s in range(n_seg):
            hin_ref[s:s + 1, re] = hr
            hin_ref[s:s + 1, im] = hi
            hr, hi = (pr * hr - pi * hi + sfin_ref[s:s + 1, re],
                      pr * hi + pi * hr + sfin_ref[s:s + 1, im])


def s5_chain(sfin, a_blk, seg_len, nq):
    n_seg, wtot = sfin.shape
    return pl.pallas_call(
        functools.partial(_s5_chain_kernel, seg_len=seg_len, n_seg=n_seg, nq=nq, gw=wtot // nq // 2),
        out_shape=jax.ShapeDtypeStruct((n_seg, wtot), F32), name="s5_chain",
    )(sfin, a_blk)


def _conv_kernel(val_ref, gate_ref, cache_ref, w_ref, b_ref, lng_ref, lnb_ref, *rest, t, width):
    y_ref, cout_ref, xp_scr = rest[-3:]
    c = pl.program_id(1)

    @pl.when(c == 0)
    def _():
        xp_scr[0:CACHE_PAD, :] = cache_ref[0]

    @pl.when(c > 0)
    def _():
        xp_scr[0:CACHE_PAD, :] = xp_scr[t:t + CACHE_PAD, :]

    xp_scr[CACHE_PAD:CACHE_PAD + t, :] = val_ref[...].astype(F32) * jax.nn.sigmoid(gate_ref[...].astype(F32))
    first = CACHE_PAD - (width - 1)
    acc = w_ref[0:1, :] * xp_scr[first:first + t, :]
    for j in range(1, width):
        acc = acc + w_ref[j:j + 1, :] * xp_scr[first + j:first + j + t, :]
    y = acc + b_ref[...]
    mu = jnp.mean(y, axis=-1, keepdims=True)
    yc = y - mu
    var = jnp.mean(yc * yc, axis=-1, keepdims=True)
    y = yc * lax.rsqrt(var + EPS) * lng_ref[...] + lnb_ref[...]
    y_ref[...] = (y * jax.nn.sigmoid(y)).astype(BF16)
    cout_ref[0] = xp_scr[t:t + CACHE_PAD, :]


def conv_module(z, cache, w, b, lng, lnb, y_prev, row_off, bsz, seq, col_val, col_gate):
    width, d_c = w.shape
    alias = [] if y_prev is None else [y_prev]
    t = min(256, seq)
    assert t >= CACHE_PAD
    nc = seq // t
    r0 = row_off // t
    rmap = lambda bb, c: r0 + bb * nc + c
    vec = pl.BlockSpec((1, d_c), lambda bb, c: (0, 0))
    return pl.pallas_call(
        functools.partial(_conv_kernel, t=t, width=width),
        grid=(bsz, nc),
        input_output_aliases={7: 0} if alias else {},
        in_specs=[pl.BlockSpec((t, d_c), lambda bb, c: (rmap(bb, c), col_val)),
                  pl.BlockSpec((t, d_c), lambda bb, c: (rmap(bb, c), col_gate)),
                  pl.BlockSpec((1, CACHE_PAD, d_c), lambda bb, c: (bb, 0, 0)),
                  pl.BlockSpec((width, d_c), lambda bb, c: (0, 0)),
                  vec, vec, vec] + [pl.BlockSpec(memory_space=pl.ANY)] * len(alias),
        out_specs=[pl.BlockSpec((t, d_c), lambda bb, c: (rmap(bb, c), 0)),
                   pl.BlockSpec((1, CACHE_PAD, d_c), lambda bb, c: (bb, 0, 0))],
        out_shape=[jax.ShapeDtypeStruct((z.shape[0], d_c), BF16),
                   jax.ShapeDtypeStruct((bsz, CACHE_PAD, d_c), F32)],
        scratch_shapes=[pltpu.VMEM((CACHE_PAD + t, d_c), F32)],
        compiler_params=_params("parallel", "arbitrary"), name="conv",
    )(z, z, cache, w, b, lng, lnb, *alias)


def _block_diag(w, gq):
    g, r, c = w.shape
    eye = jnp.eye(gq, dtype=w.dtype)
    w = w.reshape(g // gq, gq, r, c)
    return jnp.einsum("qgrc,gh->qgrhc", w, eye).reshape(g // gq, gq * r, gq * c)


def _state_to_lanes(re, im, gq):
    s, g, p = re.shape
    both = jnp.stack([re.reshape(s, g // gq, gq * p), im.reshape(s, g // gq, gq * p)], axis=2)
    return both.reshape(s, -1)


def _lanes_to_state(h, g, p, gq):
    s = h.shape[0]
    both = h.reshape(s, g // gq, 2, gq * p)
    return both[:, :, 0].reshape(s, g, p), both[:, :, 1].reshape(s, g, p)


def _to_step_major(x, n_seq):
    rows, d = x.shape
    return x.reshape(n_seq, rows // n_seq, d).swapaxes(0, 1).reshape(rows, d)


def _from_step_major(x, n_seq):
    rows, d = x.shape
    return x.reshape(rows // n_seq, n_seq, d).swapaxes(0, 1).reshape(rows, d)


PROMPT_SEGMENTS = 32


def kernel(x_prompt, x_sample, p_prompt, p_sample, state_mlstm_c, state_mlstm_n, state_mlstm_m, state_s5_re, state_s5_im, cache_conv, g_pre_mix, w_in, b_igate, b_fgate, g_mh, s5_lam_re, s5_lam_im, s5_log_dt, s5_b_re, s5_b_im, s5_c_re, s5_c_im, s5_d, s5_w_glu, conv_w, conv_b, conv_ln_g, conv_ln_b, w_out, g_post_mix, g_pre_ffn, w_ffn_gate, w_ffn_up, w_ffn_down, g_post_ffn, w_ple, w_ple_gate, g_post_ple):
    depth = w_in.shape[0]
    bp, lp, d_model = x_prompt.shape
    bs, ls, _ = x_sample.shape
    n_heads = b_igate.shape[1]
    d_a = g_mh.shape[1]
    dh = d_a // n_heads
    g_b, p_b, j_b = s5_b_re.shape[1:]
    d_b = g_b * j_b
    width, d_c = conv_w.shape[1:]
    d_ff = w_ffn_gate.shape[2]
    assert bp == 1 and 2 * n_heads <= GATE_PAD and d_b == d_c and d_a % d_b == 0
    gq = 256 // j_b
    nq = g_b // gq
    np_rows, ns_rows = bp * lp, bs * ls
    n_seg = min(PROMPT_SEGMENTS, lp // 8)
    seg_len = lp // n_seg
    col_u = 4 * d_a // d_b
    col_val, col_gate = col_u + 1, col_u + 2
    d_ff_pad = -(-d_ff // 512) * 512

    x = jnp.concatenate([x_prompt.reshape(np_rows, d_model), x_sample.reshape(ns_rows, d_model)], axis=0)
    row = lambda v: v.reshape(1, -1).astype(F32)
    zeros = lambda *s: jnp.zeros(s, F32)
    pad_m = lambda m: jnp.pad(m, ((0, 0), (0, LANES - n_heads)))[:, None, :]
    pad_cache = lambda c: jnp.pad(c, ((0, 0), (CACHE_PAD - (width - 1), 0), (0, 0)))

    n_qkvo = 4 * d_a
    w_ifs = [jnp.pad(w_in[i][:, n_qkvo:n_qkvo + 2 * n_heads], ((0, 0), (0, GATE_PAD - 2 * n_heads))).astype(BF16)
             for i in range(depth)]
    b_ifs = [jnp.pad(jnp.concatenate([b_igate[i], b_fgate[i]]), (0, GATE_PAD - 2 * n_heads)).reshape(1, GATE_PAD)
             for i in range(depth)]
    outs = {k: [] for k in ("pc", "pn", "pm", "pre", "pim", "pcv", "sc", "sn", "sm", "sre", "sim", "scv")}
    for i in range(depth):
        w_main = jnp.concatenate([w_in[i][:, :n_qkvo], w_in[i][:, n_qkvo + 2 * n_heads:]], axis=1).astype(BF16)
        w_gate = jnp.pad(w_ffn_gate[i], ((0, 0), (0, d_ff_pad - d_ff))).astype(BF16)
        w_up = jnp.pad(w_ffn_up[i], ((0, 0), (0, d_ff_pad - d_ff))).astype(BF16)
        w_down = jnp.pad(w_ffn_down[i], ((0, d_ff_pad - d_ff), (0, 0))).astype(BF16)

        ab_re, ab_im, bb_re, bb_im = s5_discretise(s5_lam_re[i], s5_lam_im[i], s5_log_dt[i], s5_b_re[i], s5_b_im[i])
        bblk = jnp.concatenate([_block_diag(bb_re, gq), _block_diag(bb_im, gq)], axis=2).astype(BF16)
        c_re_t = jnp.swapaxes(s5_c_re[i], 1, 2)
        c_im_t = jnp.swapaxes(s5_c_im[i], 1, 2)
        cblk = jnp.concatenate([_block_diag(c_re_t, gq), _block_diag(-c_im_t, gq)], axis=1).astype(BF16)
        a_blk = _state_to_lanes(ab_re[None], ab_im[None], gq)
        wglu = s5_w_glu[i].astype(BF16)
        d_row = row(s5_d[i])

        if i == 0:
            h, gates = rms_gates(x, row(g_pre_mix[i]), w_ifs[i], b_ifs[i], n_heads)
        z = mm(h, w_main, BF16)

        y_a, c1, n1, m1 = mlstm(z, gates, zeros(bp, n_heads, dh, dh), zeros(bp, n_heads, dh),
                                zeros(bp, 1, LANES), row(g_mh[i]), None, 0, bp, lp, n_heads, dh)
        y_a, c2, n2, m2 = mlstm(z, gates, state_mlstm_c[i], state_mlstm_n[i], pad_m(state_mlstm_m[i]),
                                row(g_mh[i]), y_a, np_rows, bs, ls, n_heads, dh)

        u_all = z[:, 4 * d_a:4 * d_a + d_b]
        u_p = _to_step_major(u_all[:np_rows], n_seg)
        u_s = _to_step_major(u_all[np_rows:], bs)
        wtot = a_blk.shape[1]
        (sfin,) = s5_scan(u_p, zeros(n_seg, wtot), a_blk, bblk, None, None, None, n_seg, False)
        hin = s5_chain(sfin, a_blk, seg_len, nq)
        yb_p, hfin_p = s5_scan(u_p, hin, a_blk, bblk, cblk, d_row, wglu, n_seg, True)
        yb_s, hfin_s = s5_scan(u_s, _state_to_lanes(state_s5_re[i], state_s5_im[i], gq), a_blk, bblk, cblk,
                               d_row, wglu, bs, True)
        y_b = jnp.concatenate([_from_step_major(yb_p, n_seg), _from_step_major(yb_s, bs)], axis=0)
        re_p, im_p = _lanes_to_state(hfin_p[n_seg - 1:], g_b, p_b, gq)
        re_s, im_s = _lanes_to_state(hfin_s, g_b, p_b, gq)

        cw, cb, lg, lb = conv_w[i], row(conv_b[i]), row(conv_ln_g[i]), row(conv_ln_b[i])
        y_c, cv_p = conv_module(z, zeros(bp, CACHE_PAD, d_c), cw, cb, lg, lb, None, 0, bp, lp, col_val, col_gate)
        y_c, cv_s = conv_module(z, pad_cache(cache_conv[i]), cw, cb, lg, lb, y_c, np_rows, bs, ls, col_val, col_gate)

        x, h = mm_res([y_a, y_b, y_c], w_out[i].astype(BF16), x, row(g_post_mix[i]), split="n",
                      second="norm", g2=row(g_pre_ffn[i]), name="mm_out")

        f = mm_swiglu(h, w_gate, w_up)
        x, xb = mm_res([f], w_down, x, row(g_post_ffn[i]), split="k", step_pref=1024, second="cast",
                       name="mm_down")

        pe = jnp.concatenate([p_prompt[i].reshape(np_rows, -1), p_sample[i].reshape(ns_rows, -1)], axis=0).astype(BF16)
        ple_args = dict(split="n", pe=pe, wple=w_ple[i].astype(BF16), name="mm_ple")
        if i + 1 < depth:
            x, h, gates = mm_res([xb], w_ple_gate[i].astype(BF16), x, row(g_post_ple[i]), second="norm",
                                 g2=row(g_pre_mix[i + 1]), wif=w_ifs[i + 1], bif=b_ifs[i + 1], n_heads=n_heads,
                                 **ple_args)
        else:
            (x,) = mm_res([xb], w_ple_gate[i].astype(BF16), x, row(g_post_ple[i]), **ple_args)

        first = CACHE_PAD - (width - 1)
        outs["pc"].append(c1); outs["pn"].append(n1); outs["pm"].append(m1[:, 0, :n_heads])
        outs["pre"].append(re_p); outs["pim"].append(im_p); outs["pcv"].append(cv_p[:, first:])
        outs["sc"].append(c2); outs["sn"].append(n2); outs["sm"].append(m2[:, 0, :n_heads])
        outs["sre"].append(re_s); outs["sim"].append(im_s); outs["scv"].append(cv_s[:, first:])

    st = lambda k: jnp.stack(outs[k])
    return (x[:np_rows].reshape(bp, lp, d_model), x[np_rows:].reshape(bs, ls, d_model),
            st("pc"), st("pn"), st("pm"), st("pre"), st("pim"), st("pcv"),
            st("sc"), st("sn"), st("sm"), st("sre"), st("sim"), st("scv"))
```

```python
import functools
import math

import jax
import jax.numpy as jnp
from jax import lax
from jax.experimental import pallas as pl
from jax.experimental.pallas import tpu as pltpu

F32 = jnp.float32
BF16 = jnp.bfloat16
EPS = 1e-6
LANES = 128
VMEM_LIMIT = 56 * 1024 * 1024
GATE_PAD = LANES
CACHE_PAD = 32


def _pick(n, pref):
    t = min(pref, n)
    while n % t:
        t //= 2
    return t


def _params(*sem):
    return pltpu.CompilerParams(dimension_semantics=sem, vmem_limit_bytes=VMEM_LIMIT)


def _rms(x, g):
    return x * lax.rsqrt(jnp.mean(x * x, axis=-1, keepdims=True) + EPS) * g


def _log_sigmoid(x):
    return jnp.minimum(x, 0.0) - jnp.log1p(jnp.exp(-jnp.abs(x)))


def _dot(a, b):
    return jnp.dot(a, b, preferred_element_type=F32)


def _gate_act(gt, n_heads):
    lane = lax.broadcasted_iota(jnp.int32, gt.shape, 1)
    return jnp.where(lane >= n_heads, _log_sigmoid(gt), gt)


def _rms_gates_kernel(x_ref, g_ref, wif_ref, bias_ref, h_ref, gate_ref, *, n_heads):
    hb = _rms(x_ref[...], g_ref[...]).astype(BF16)
    h_ref[...] = hb
    gate_ref[...] = _gate_act(_dot(hb, wif_ref[...]) + bias_ref[...], n_heads)


def rms_gates(x, g, wif, bias, n_heads):
    n, d = x.shape
    tm = _pick(n, 256)
    return pl.pallas_call(
        functools.partial(_rms_gates_kernel, n_heads=n_heads),
        grid=(n // tm,),
        in_specs=[pl.BlockSpec((tm, d), lambda i: (i, 0)),
                  pl.BlockSpec((1, d), lambda i: (0, 0)),
                  pl.BlockSpec((d, GATE_PAD), lambda i: (0, 0)),
                  pl.BlockSpec((1, GATE_PAD), lambda i: (0, 0))],
        out_specs=[pl.BlockSpec((tm, d), lambda i: (i, 0)),
                   pl.BlockSpec((tm, GATE_PAD), lambda i: (i, 0))],
        out_shape=[jax.ShapeDtypeStruct((n, d), BF16),
                   jax.ShapeDtypeStruct((n, GATE_PAD), F32)],
        compiler_params=_params("parallel"), name="rms_gates",
    )(x, g, wif, bias)


def _mm_kernel(a_ref, b_ref, o_ref):
    o_ref[...] = _dot(a_ref[...], b_ref[...]).astype(o_ref.dtype)


def mm(a, b, out_dtype):
    m, k = a.shape
    n = b.shape[1]
    tm, tn = _pick(m, 1024), _pick(n, 1024)
    return pl.pallas_call(
        _mm_kernel,
        grid=(m // tm, n // tn),
        in_specs=[pl.BlockSpec((tm, k), lambda i, j: (i, 0)),
                  pl.BlockSpec((k, tn), lambda i, j: (0, j))],
        out_specs=pl.BlockSpec((tm, tn), lambda i, j: (i, j)),
        out_shape=jax.ShapeDtypeStruct((m, n), out_dtype),
        compiler_params=_params("parallel", "arbitrary"), name="mm_in",
    )(a, b)


def _mm_swiglu_kernel(a_ref, bg_ref, bu_ref, o_ref):
    a = a_ref[...]
    gate = _dot(a, bg_ref[...])
    up = _dot(a, bu_ref[...])
    o_ref[...] = (gate * jax.nn.sigmoid(gate) * up).astype(o_ref.dtype)


def mm_swiglu(a, bg, bu):
    m, k = a.shape
    n = bg.shape[1]
    tm, tn = _pick(m, 1024), _pick(n, 512)
    return pl.pallas_call(
        _mm_swiglu_kernel,
        grid=(m // tm, n // tn),
        in_specs=[pl.BlockSpec((tm, k), lambda i, j: (i, 0)),
                  pl.BlockSpec((k, tn), lambda i, j: (0, j)),
                  pl.BlockSpec((k, tn), lambda i, j: (0, j))],
        out_specs=pl.BlockSpec((tm, tn), lambda i, j: (i, j)),
        out_shape=jax.ShapeDtypeStruct((m, n), BF16),
        compiler_params=_params("parallel", "arbitrary"), name="mm_swiglu",
    )(a, bg, bu)


EPI_CHUNKS = 8
MXU_DEPTH = 256


def _mm_res_kernel(*refs, n_a, split, n_tiles, n_steps, tn, rb, ple, second, gates, n_heads):
    refs = list(refs)
    a_refs = [refs.pop(0) for _ in range(n_a)]
    b_ref, x_ref, g_ref = refs.pop(0), refs.pop(0), refs.pop(0)
    g2_ref = refs.pop(0) if second == "norm" else None
    pe_ref, wple_ref = (refs.pop(0), refs.pop(0)) if ple else (None, None)
    wif_ref, bif_ref = (refs.pop(0), refs.pop(0)) if gates else (None, None)
    xo_ref = refs.pop(0)
    h_ref = refs.pop(0) if second else None
    gate_ref = refs.pop(0) if gates else None
    acc_refs = (refs.pop(0), refs.pop(0))
    i, s = pl.program_id(0), pl.program_id(1)

    def matmul(acc_ref):
        if split == "n":
            part, k0 = None, 0
            for a_ref in a_refs:
                kw = a_ref.shape[1]
                d = _dot(a_ref[...], b_ref[k0:k0 + kw, :])
                part = d if part is None else part + d
                k0 += kw
            if ple:
                part = _dot(pe_ref[...], wple_ref[...]) * jax.nn.sigmoid(part)
            acc_ref[:, pl.ds(pl.multiple_of(s * tn, tn), tn)] = part
        else:
            acc_ref[...] += _dot(a_refs[0][...], b_ref[...])

    def epilogue(acc_ref):
        rows = pl.ds(pl.multiple_of(s * rb, rb), rb)
        f = acc_ref[rows, :]
        if split == "k":
            acc_ref[rows, :] = jnp.zeros_like(f)
        xn = x_ref[...] + _rms(f, g_ref[...])
        xo_ref[...] = xn
        if second == "norm":
            hb = _rms(xn, g2_ref[...]).astype(BF16)
            h_ref[...] = hb
            if gates:
                gate_ref[...] = _gate_act(_dot(hb, wif_ref[...]) + bif_ref[...], n_heads)
        elif second == "cast":
            h_ref[...] = xn.astype(BF16)

    @pl.when((i == 0) & (s == 0))
    def _():
        for acc_ref in acc_refs:
            acc_ref[...] = jnp.zeros_like(acc_ref)

    for parity in range(2):
        mine = (i < n_tiles) & (lax.rem(i, 2) == parity)

        @pl.when(mine & (s < EPI_CHUNKS))
        def _(parity=parity):
            epilogue(acc_refs[1 - parity])
            matmul(acc_refs[parity])

        if n_steps > EPI_CHUNKS:
            @pl.when(mine & (s >= EPI_CHUNKS))
            def _(parity=parity):
                matmul(acc_refs[parity])

    @pl.when((i == n_tiles) & (s < EPI_CHUNKS))
    def _():
        epilogue(acc_refs[1 - n_tiles % 2])


def mm_res(a_parts, b, x, g, *, split, second=None, g2=None, pe=None, wple=None, wif=None, bif=None,
           n_heads=0, tm_pref=512, name="mm_res"):
    m = x.shape[0]
    kdim, n = b.shape
    tm = _pick(m, tm_pref)
    n_tiles, nch = m // tm, EPI_CHUNKS
    rb = tm // nch
    ple, gates = pe is not None, wif is not None
    once = pl.Buffered(1)
    const = lambda shape: pl.BlockSpec(shape, lambda i, s: (0, 0), pipeline_mode=once)
    tile = lambda i: jnp.minimum(i, n_tiles - 1)
    chunk = lambda i, s: jnp.where(i == 0, 0, (i - 1) * nch + jnp.minimum(s, nch - 1))
    chunk_spec = lambda w: pl.BlockSpec((rb, w), lambda i, s: (chunk(i, s), 0))
    if split == "n":
        n_steps = nch
        tn = n // n_steps
        in_specs = [pl.BlockSpec((tm, a.shape[1]), lambda i, s: (tile(i), 0)) for a in a_parts]
        in_specs.append(pl.BlockSpec((kdim, tn), lambda i, s: (0, s)))
    else:
        (a,) = a_parts
        tn = n
        tk = max(t for t in range(MXU_DEPTH, kdim // nch + 1, MXU_DEPTH) if kdim % t == 0)
        n_steps = kdim // tk
        in_specs = [pl.BlockSpec((tm, tk), lambda i, s: (tile(i), s)),
                    pl.BlockSpec((tk, n), lambda i, s: (s, 0))]
    in_specs += [chunk_spec(n), const((1, n))]
    args = list(a_parts) + [b, x, g]
    if second == "norm":
        in_specs.append(const((1, n)))
        args.append(g2)
    if ple:
        assert split == "n"
        in_specs += [pl.BlockSpec((tm, pe.shape[1]), lambda i, s: (tile(i), 0)),
                     pl.BlockSpec((wple.shape[0], tn), lambda i, s: (0, s))]
        args += [pe, wple]
    if gates:
        in_specs += [const(wif.shape), const(bif.shape)]
        args += [wif, bif]
    out_specs, out_shape = [chunk_spec(n)], [jax.ShapeDtypeStruct((m, n), F32)]
    if second:
        out_specs.append(chunk_spec(n))
        out_shape.append(jax.ShapeDtypeStruct((m, n), BF16))
    if gates:
        out_specs.append(chunk_spec(GATE_PAD))
        out_shape.append(jax.ShapeDtypeStruct((m, GATE_PAD), F32))
    return pl.pallas_call(
        functools.partial(_mm_res_kernel, n_a=len(a_parts), split=split, n_tiles=n_tiles, n_steps=n_steps, tn=tn,
                          rb=rb, ple=ple,
                          second=second, gates=gates, n_heads=n_heads),
        grid=(n_tiles + 1, n_steps),
        in_specs=in_specs, out_specs=out_specs, out_shape=out_shape,
        scratch_shapes=[pltpu.VMEM((tm, n), F32), pltpu.VMEM((tm, n), F32)],
        compiler_params=_params("arbitrary", "arbitrary"), name=name,
    )(*args)


def _split3(x):
    hi = x.astype(BF16)
    r1 = x - hi.astype(F32)
    mid = r1.astype(BF16)
    lo = (r1 - mid.astype(F32)).astype(BF16)
    return hi, mid, lo


def _mlstm_kernel(zq_ref, zk_ref, zv_ref, zo_ref, gate_ref, c0_ref, n0_ref, m0_ref,
                  gmh_ref, *rest, n_heads, dh, t):
    y_ref, c_ref, n_ref, m_ref = rest[-4:]

    @pl.when(pl.program_id(1) == 0)
    def _():
        c_ref[...] = c0_ref[...]
        n_ref[...] = n0_ref[...]
        m_ref[...] = m0_ref[...]

    row = lax.broadcasted_iota(jnp.int32, (t, t), 0)
    col = lax.broadcasted_iota(jnp.int32, (t, t), 1)
    causal = row >= col
    tri = causal.astype(BF16)
    tri_t = (col >= row).astype(BF16)
    g_col = gate_ref[...]
    g_row = g_col.T
    cum_col = sum(_dot(tri, p) for p in _split3(g_col))
    cum_row = sum(_dot(p, tri_t) for p in _split3(g_row))
    scale = dh ** -0.5

    for h in range(n_heads):
        sl = slice(h * dh, (h + 1) * dh)
        q = zq_ref[:, sl]
        k = zk_ref[:, sl]
        v = zv_ref[:, sl]
        i_col = g_col[:, h:h + 1]
        i_row = g_row[h:h + 1, :]
        b_col = cum_col[:, n_heads + h:n_heads + h + 1]
        b_row = cum_row[n_heads + h:n_heads + h + 1, :]
        m_prev = m_ref[0, :, h:h + 1]
        c_prev = c_ref[0, h]
        n_prev = n_ref[0, h:h + 1, :]

        d = jnp.where(causal, b_col - b_row + i_row, -jnp.inf)
        inter = b_col + m_prev
        mt = jnp.maximum(inter, jnp.max(d, axis=1, keepdims=True))
        s = lax.dot_general(q, k, (((1,), (1,)), ((), ())), preferred_element_type=F32)
        s = s * scale * jnp.exp(d - mt)
        sc_t = jnp.exp(inter - mt)
        num = sc_t * _dot(q, c_prev.astype(BF16)) + _dot(s.astype(BF16), v)
        qn = jnp.sum(q.astype(F32) * n_prev, axis=1, keepdims=True)
        den = sc_t * qn + jnp.sum(s, axis=1, keepdims=True)
        hh = num / jnp.maximum(jnp.abs(den), jnp.exp(-mt))

        b_last = b_col[t - 1:t, :]
        m_new = mt[t - 1:t, :]
        wl = jnp.exp(b_last - b_col + i_col - m_new)
        sc = jnp.exp(b_last + m_prev - m_new)
        kw = k.astype(F32) * wl
        kv = lax.dot_general(kw.astype(BF16), v, (((0,), (0,)), ((), ())), preferred_element_type=F32)
        c_ref[0, h] = sc * c_prev + kv * scale
        n_ref[0, h:h + 1, :] = sc * n_prev + jnp.sum(kw, axis=0, keepdims=True) * scale
        m_ref[0, :, h:h + 1] = m_new

        hn = _rms(hh, gmh_ref[:, sl])
        y_ref[:, sl] = (hn * jax.nn.sigmoid(zo_ref[:, sl].astype(F32))).astype(BF16)


def mlstm(z, gates, c0, n0, m0, gmh, y_prev, row_off, bsz, seq, n_heads, dh):
    d_a = n_heads * dh
    n_rows = z.shape[0]
    t = min(256, seq)
    nc = seq // t
    r0 = row_off // t
    rmap = lambda b, c: r0 + b * nc + c
    zspec = lambda j: pl.BlockSpec((t, d_a), lambda b, c, j=j: (rmap(b, c), j))
    alias = [] if y_prev is None else [y_prev]
    return pl.pallas_call(
        functools.partial(_mlstm_kernel, n_heads=n_heads, dh=dh, t=t),
        grid=(bsz, nc),
        input_output_aliases={9: 0} if alias else {},
        in_specs=[zspec(0), zspec(1), zspec(2), zspec(3),
                  pl.BlockSpec((t, GATE_PAD), lambda b, c: (rmap(b, c), 0)),
                  pl.BlockSpec((1, n_heads, dh, dh), lambda b, c: (b, 0, 0, 0)),
                  pl.BlockSpec((1, n_heads, dh), lambda b, c: (b, 0, 0)),
                  pl.BlockSpec((1, 1, LANES), lambda b, c: (b, 0, 0)),
                  pl.BlockSpec((1, d_a), lambda b, c: (0, 0))]
        + [pl.BlockSpec(memory_space=pl.ANY)] * len(alias),
        out_specs=[pl.BlockSpec((t, d_a), lambda b, c: (rmap(b, c), 0)),
                   pl.BlockSpec((1, n_heads, dh, dh), lambda b, c: (b, 0, 0, 0)),
                   pl.BlockSpec((1, n_heads, dh), lambda b, c: (b, 0, 0)),
                   pl.BlockSpec((1, 1, LANES), lambda b, c: (b, 0, 0))],
        out_shape=[jax.ShapeDtypeStruct((n_rows, d_a), BF16),
                   jax.ShapeDtypeStruct((bsz, n_heads, dh, dh), F32),
                   jax.ShapeDtypeStruct((bsz, n_heads, dh), F32),
                   jax.ShapeDtypeStruct((bsz, 1, LANES), F32)],
        compiler_params=_params("parallel", "arbitrary"), name="mlstm",
    )(z, z, z, z, gates, c0, n0, m0, gmh, *alias)


def _s5_disc_kernel(lr_ref, li_ref, ldt_ref, br_ref, bi_ref, abr_ref, abi_ref, bbr_ref, bbi_ref):
    dt = jnp.exp(ldt_ref[...])
    lr = lr_ref[...]
    li = li_ref[...]
    mag = jnp.exp(lr * dt)
    ab_re = mag * jnp.cos(li * dt)
    ab_im = mag * jnp.sin(li * dt)
    den = lr * lr + li * li
    nr = ab_re - 1.0
    coef_re = (nr * lr + ab_im * li) / den
    coef_im = (ab_im * lr - nr * li) / den
    br = br_ref[...]
    bi = bi_ref[...]
    abr_ref[...] = ab_re
    abi_ref[...] = ab_im
    bbr_ref[...] = coef_re * br - coef_im * bi
    bbi_ref[...] = coef_re * bi + coef_im * br


def s5_discretise(lam_re, lam_im, log_dt, b_re, b_im):
    g, p, j = b_re.shape
    rep = lambda a: jnp.repeat(a, j, axis=0)
    br_t = jnp.swapaxes(b_re, 1, 2).reshape(g * j, p)
    bi_t = jnp.swapaxes(b_im, 1, 2).reshape(g * j, p)
    shp = jax.ShapeDtypeStruct((g * j, p), F32)
    abr, abi, bbr, bbi = pl.pallas_call(
        _s5_disc_kernel, out_shape=[shp, shp, shp, shp], name="s5_disc",
    )(rep(lam_re), rep(lam_im), rep(log_dt[:, None]), br_t, bi_t)
    first = lambda a: a.reshape(g, j, p)[:, 0]
    return first(abr), first(abi), bbr.reshape(g, j, p), bbi.reshape(g, j, p)


def _s5_scan_kernel(*refs, s_rows, tc, nq, gw, slab, emit_y):
    if emit_y:
        u_ref, h0_ref, a_ref, bblk_ref, cblk_ref, d_ref, wglu_ref, y_ref, hout_ref, bu_scr = refs
    else:
        u_ref, h0_ref, a_ref, bblk_ref, hout_ref, bu_scr = refs

    @pl.when(pl.program_id(0) == 0)
    def _():
        hout_ref[...] = h0_ref[...]

    u = u_ref[...]
    kq = u.shape[1] // nq
    for q in range(nq):
        bu_scr[:, q * 2 * gw:(q + 1) * 2 * gw] = _dot(u[:, q * kq:(q + 1) * kq], bblk_ref[q])

    for q in range(nq):
        for lo in range(0, gw, slab):
            re = slice(q * 2 * gw + lo, q * 2 * gw + lo + slab)
            im = slice(q * 2 * gw + gw + lo, q * 2 * gw + gw + lo + slab)
            ar = a_ref[:, re]
            ai = a_ref[:, im]

            def body(step, carry, re=re, im=im, ar=ar, ai=ai):
                hr, hi = carry
                rows = pl.ds(pl.multiple_of(step * s_rows, s_rows), s_rows)
                nr = ar * hr - ai * hi + bu_scr[rows, re]
                ni = ar * hi + ai * hr + bu_scr[rows, im]
                bu_scr[rows, re] = nr
                bu_scr[rows, im] = ni
                return nr, ni

            hr, hi = lax.fori_loop(0, tc, body, (hout_ref[:, re], hout_ref[:, im]))
            hout_ref[:, re] = hr
            hout_ref[:, im] = hi

    if emit_y:
        ys = [_dot(bu_scr[:, q * 2 * gw:(q + 1) * 2 * gw].astype(BF16), cblk_ref[q]) for q in range(nq)]
        y = jnp.concatenate(ys, axis=1) + d_ref[...] * u.astype(F32)
        zg = 0.5 * y * (1.0 + jnp.tanh(math.sqrt(2.0 / math.pi) * (y + 0.044715 * (y * y * y))))
        y_ref[...] = (zg * jax.nn.sigmoid(_dot(zg.astype(BF16), wglu_ref[...]))).astype(BF16)


def s5_scan(u_perm, h0, a_blk, bblk, cblk, d, wglu, s_rows, emit_y):
    rows, d_b = u_perm.shape
    nq, kq, w2 = bblk.shape
    gw = w2 // 2
    steps = rows // s_rows
    tc = _pick(steps, max(1, 256 // s_rows))
    rb = tc * s_rows
    wtot = nq * w2
    const2 = lambda c: (0, 0)
    in_specs = [pl.BlockSpec((rb, d_b), lambda c: (c, 0)),
                pl.BlockSpec((s_rows, wtot), const2),
                pl.BlockSpec((1, wtot), const2),
                pl.BlockSpec((nq, kq, w2), lambda c: (0, 0, 0))]
    args = [u_perm, h0, a_blk, bblk]
    out_specs = [pl.BlockSpec((s_rows, wtot), const2)]
    out_shape = [jax.ShapeDtypeStruct((s_rows, wtot), F32)]
    if emit_y:
        in_specs += [pl.BlockSpec((nq, w2, kq), lambda c: (0, 0, 0)),
                     pl.BlockSpec((1, d_b), const2),
                     pl.BlockSpec((d_b, d_b), const2)]
        args += [cblk, d, wglu]
        out_specs = [pl.BlockSpec((rb, d_b), lambda c: (c, 0))] + out_specs
        out_shape = [jax.ShapeDtypeStruct((rows, d_b), BF16)] + out_shape
    return pl.pallas_call(
        functools.partial(_s5_scan_kernel, s_rows=s_rows, tc=tc, nq=nq, gw=gw,
                          slab=min(512, gw), emit_y=emit_y),
        grid=(steps // tc,),
        in_specs=in_specs, out_specs=out_specs, out_shape=out_shape,
        scratch_shapes=[pltpu.VMEM((rb, wtot), F32)],
        compiler_params=_params("arbitrary"), name="s5_scan" if emit_y else "s5_ends",
    )(*args)


def _s5_chain_kernel(sfin_ref, a_ref, hin_ref, *, seg_len, n_seg, nq, gw):
    for q in range(nq):
        re = slice(q * 2 * gw, q * 2 * gw + gw)
        im = slice(q * 2 * gw + gw, (q + 1) * 2 * gw)
        br, bi = a_ref[:, re], a_ref[:, im]
        pr, pi = jnp.ones_like(br), jnp.zeros_like(br)
        e = seg_len
        while e:
            if e & 1:
                pr, pi = pr * br - pi * bi, pr * bi + pi * br
            br, bi = br * br - bi * bi, 2.0 * br * bi
            e >>= 1
        hr, hi = jnp.zeros_like(pr), jnp.zeros_like(pr)
        for s in range(n_seg):
            hin_ref[s:s + 1, re] = hr
            hin_ref[s:s + 1, im] = hi
            hr, hi = (pr * hr - pi * hi + sfin_ref[s:s + 1, re],
                      pr * hi + pi * hr + sfin_ref[s:s + 1, im])


def s5_chain(sfin, a_blk, seg_len, nq):
    n_seg, wtot = sfin.shape
    return pl.pallas_call(
        functools.partial(_s5_chain_kernel, seg_len=seg_len, n_seg=n_seg, nq=nq, gw=wtot // nq // 2),
        out_shape=jax.ShapeDtypeStruct((n_seg, wtot), F32), name="s5_chain",
    )(sfin, a_blk)


def _conv_kernel(val_ref, gate_ref, cache_ref, w_ref, b_ref, lng_ref, lnb_ref, *rest, t, width):
    y_ref, cout_ref, xp_scr = rest[-3:]
    c = pl.program_id(1)

    @pl.when(c == 0)
    def _():
        xp_scr[0:CACHE_PAD, :] = cache_ref[0]

    @pl.when(c > 0)
    def _():
        xp_scr[0:CACHE_PAD, :] = xp_scr[t:t + CACHE_PAD, :]

    xp_scr[CACHE_PAD:CACHE_PAD + t, :] = val_ref[...].astype(F32) * jax.nn.sigmoid(gate_ref[...].astype(F32))
    first = CACHE_PAD - (width - 1)
    acc = w_ref[0:1, :] * xp_scr[first:first + t, :]
    for j in range(1, width):
        acc = acc + w_ref[j:j + 1, :] * xp_scr[first + j:first + j + t, :]
    y = acc + b_ref[...]
    mu = jnp.mean(y, axis=-1, keepdims=True)
    yc = y - mu
    var = jnp.mean(yc * yc, axis=-1, keepdims=True)
    y = yc * lax.rsqrt(var + EPS) * lng_ref[...] + lnb_ref[...]
    y_ref[...] = (y * jax.nn.sigmoid(y)).astype(BF16)
    cout_ref[0] = xp_scr[t:t + CACHE_PAD, :]


def conv_module(z, cache, w, b, lng, lnb, y_prev, row_off, bsz, seq, col_val, col_gate):
    width, d_c = w.shape
    alias = [] if y_prev is None else [y_prev]
    t = min(256, seq)
    assert t >= CACHE_PAD
    nc = seq // t
    r0 = row_off // t
    rmap = lambda bb, c: r0 + bb * nc + c
    vec = pl.BlockSpec((1, d_c), lambda bb, c: (0, 0))
    return pl.pallas_call(
        functools.partial(_conv_kernel, t=t, width=width),
        grid=(bsz, nc),
        input_output_aliases={7: 0} if alias else {},
        in_specs=[pl.BlockSpec((t, d_c), lambda bb, c: (rmap(bb, c), col_val)),
                  pl.BlockSpec((t, d_c), lambda bb, c: (rmap(bb, c), col_gate)),
                  pl.BlockSpec((1, CACHE_PAD, d_c), lambda bb, c: (bb, 0, 0)),
                  pl.BlockSpec((width, d_c), lambda bb, c: (0, 0)),
                  vec, vec, vec] + [pl.BlockSpec(memory_space=pl.ANY)] * len(alias),
        out_specs=[pl.BlockSpec((t, d_c), lambda bb, c: (rmap(bb, c), 0)),
                   pl.BlockSpec((1, CACHE_PAD, d_c), lambda bb, c: (bb, 0, 0))],
        out_shape=[jax.ShapeDtypeStruct((z.shape[0], d_c), BF16),
                   jax.ShapeDtypeStruct((bsz, CACHE_PAD, d_c), F32)],
        scratch_shapes=[pltpu.VMEM((CACHE_PAD + t, d_c), F32)],
        compiler_params=_params("parallel", "arbitrary"), name="conv",
    )(z, z, cache, w, b, lng, lnb, *alias)


def _block_diag(w, gq):
    g, r, c = w.shape
    eye = jnp.eye(gq, dtype=w.dtype)
    w = w.reshape(g // gq, gq, r, c)
    return jnp.einsum("qgrc,gh->qgrhc", w, eye).reshape(g // gq, gq * r, gq * c)


def _state_to_lanes(re, im, gq):
    s, g, p = re.shape
    both = jnp.stack([re.reshape(s, g // gq, gq * p), im.reshape(s, g // gq, gq * p)], axis=2)
    return both.reshape(s, -1)


def _lanes_to_state(h, g, p, gq):
    s = h.shape[0]
    both = h.reshape(s, g // gq, 2, gq * p)
    return both[:, :, 0].reshape(s, g, p), both[:, :, 1].reshape(s, g, p)


def _to_step_major(x, n_seq):
    rows, d = x.shape
    return x.reshape(n_seq, rows // n_seq, d).swapaxes(0, 1).reshape(rows, d)


def _from_step_major(x, n_seq):
    rows, d = x.shape
    return x.reshape(rows // n_seq, n_seq, d).swapaxes(0, 1).reshape(rows, d)


PROMPT_SEGMENTS = 32


def kernel(x_prompt, x_sample, p_prompt, p_sample, state_mlstm_c, state_mlstm_n, state_mlstm_m, state_s5_re, state_s5_im, cache_conv, g_pre_mix, w_in, b_igate, b_fgate, g_mh, s5_lam_re, s5_lam_im, s5_log_dt, s5_b_re, s5_b_im, s5_c_re, s5_c_im, s5_d, s5_w_glu, conv_w, conv_b, conv_ln_g, conv_ln_b, w_out, g_post_mix, g_pre_ffn, w_ffn_gate, w_ffn_up, w_ffn_down, g_post_ffn, w_ple, w_ple_gate, g_post_ple):
    depth = w_in.shape[0]
    bp, lp, d_model = x_prompt.shape
    bs, ls, _ = x_sample.shape
    n_heads = b_igate.shape[1]
    d_a = g_mh.shape[1]
    dh = d_a // n_heads
    g_b, p_b, j_b = s5_b_re.shape[1:]
    d_b = g_b * j_b
    width, d_c = conv_w.shape[1:]
    d_ff = w_ffn_gate.shape[2]
    assert bp == 1 and 2 * n_heads <= GATE_PAD and d_b == d_c and d_a % d_b == 0
    gq = 256 // j_b
    nq = g_b // gq
    np_rows, ns_rows = bp * lp, bs * ls
    n_seg = min(PROMPT_SEGMENTS, lp // 8)
    seg_len = lp // n_seg
    col_u = 4 * d_a // d_b
    col_val, col_gate = col_u + 1, col_u + 2
    d_ff_pad = -(-d_ff // 1024) * 1024

    x = jnp.concatenate([x_prompt.reshape(np_rows, d_model), x_sample.reshape(ns_rows, d_model)], axis=0)
    row = lambda v: v.reshape(1, -1).astype(F32)
    zeros = lambda *s: jnp.zeros(s, F32)
    pad_m = lambda m: jnp.pad(m, ((0, 0), (0, LANES - n_heads)))[:, None, :]
    pad_cache = lambda c: jnp.pad(c, ((0, 0), (CACHE_PAD - (width - 1), 0), (0, 0)))

    n_qkvo = 4 * d_a
    w_ifs = [jnp.pad(w_in[i][:, n_qkvo:n_qkvo + 2 * n_heads], ((0, 0), (0, GATE_PAD - 2 * n_heads))).astype(BF16)
             for i in range(depth)]
    b_ifs = [jnp.pad(jnp.concatenate([b_igate[i], b_fgate[i]]), (0, GATE_PAD - 2 * n_heads)).reshape(1, GATE_PAD)
             for i in range(depth)]
    outs = {k: [] for k in ("pc", "pn", "pm", "pre", "pim", "pcv", "sc", "sn", "sm", "sre", "sim", "scv")}
    for i in range(depth):
        w_main = jnp.concatenate([w_in[i][:, :n_qkvo], w_in[i][:, n_qkvo + 2 * n_heads:]], axis=1).astype(BF16)
        w_gate = jnp.pad(w_ffn_gate[i], ((0, 0), (0, d_ff_pad - d_ff))).astype(BF16)
        w_up = jnp.pad(w_ffn_up[i], ((0, 0), (0, d_ff_pad - d_ff))).astype(BF16)
        w_down = jnp.pad(w_ffn_down[i], ((0, d_ff_pad - d_ff), (0, 0))).astype(BF16)

        ab_re, ab_im, bb_re, bb_im = s5_discretise(s5_lam_re[i], s5_lam_im[i], s5_log_dt[i], s5_b_re[i], s5_b_im[i])
        bblk = jnp.concatenate([_block_diag(bb_re, gq), _block_diag(bb_im, gq)], axis=2).astype(BF16)
        c_re_t = jnp.swapaxes(s5_c_re[i], 1, 2)
        c_im_t = jnp.swapaxes(s5_c_im[i], 1, 2)
        cblk = jnp.concatenate([_block_diag(c_re_t, gq), _block_diag(-c_im_t, gq)], axis=1).astype(BF16)
        a_blk = _state_to_lanes(ab_re[None], ab_im[None], gq)
        wglu = s5_w_glu[i].astype(BF16)
        d_row = row(s5_d[i])

        if i == 0:
            h, gates = rms_gates(x, row(g_pre_mix[i]), w_ifs[i], b_ifs[i], n_heads)
        z = mm(h, w_main, BF16)

        y_a, c1, n1, m1 = mlstm(z, gates, zeros(bp, n_heads, dh, dh), zeros(bp, n_heads, dh),
                                zeros(bp, 1, LANES), row(g_mh[i]), None, 0, bp, lp, n_heads, dh)
        y_a, c2, n2, m2 = mlstm(z, gates, state_mlstm_c[i], state_mlstm_n[i], pad_m(state_mlstm_m[i]),
                                row(g_mh[i]), y_a, np_rows, bs, ls, n_heads, dh)

        u_all = z[:, 4 * d_a:4 * d_a + d_b]
        u_p = _to_step_major(u_all[:np_rows], n_seg)
        u_s = _to_step_major(u_all[np_rows:], bs)
        wtot = a_blk.shape[1]
        (sfin,) = s5_scan(u_p, zeros(n_seg, wtot), a_blk, bblk, None, None, None, n_seg, False)
        hin = s5_chain(sfin, a_blk, seg_len, nq)
        yb_p, hfin_p = s5_scan(u_p, hin, a_blk, bblk, cblk, d_row, wglu, n_seg, True)
        yb_s, hfin_s = s5_scan(u_s, _state_to_lanes(state_s5_re[i], state_s5_im[i], gq), a_blk, bblk, cblk,
                               d_row, wglu, bs, True)
        y_b = jnp.concatenate([_from_step_major(yb_p, n_seg), _from_step_major(yb_s, bs)], axis=0)
        re_p, im_p = _lanes_to_state(hfin_p[n_seg - 1:], g_b, p_b, gq)
        re_s, im_s = _lanes_to_state(hfin_s, g_b, p_b, gq)

        cw, cb, lg, lb = conv_w[i], row(conv_b[i]), row(conv_ln_g[i]), row(conv_ln_b[i])
        y_c, cv_p = conv_module(z, zeros(bp, CACHE_PAD, d_c), cw, cb, lg, lb, None, 0, bp, lp, col_val, col_gate)
        y_c, cv_s = conv_module(z, pad_cache(cache_conv[i]), cw, cb, lg, lb, y_c, np_rows, bs, ls, col_val, col_gate)

        x, h = mm_res([y_a, y_b, y_c], w_out[i].astype(BF16), x, row(g_post_mix[i]), split="n",
                      second="norm", g2=row(g_pre_ffn[i]), name="mm_out")

        f = mm_swiglu(h, w_gate, w_up)
        x, xb = mm_res([f], w_down, x, row(g_post_ffn[i]), split="k", second="cast",
                       name="mm_down")

        pe = jnp.concatenate([p_prompt[i].reshape(np_rows, -1), p_sample[i].reshape(ns_rows, -1)], axis=0).astype(BF16)
        ple_args = dict(split="n", pe=pe, wple=w_ple[i].astype(BF16), name="mm_ple")
        if i + 1 < depth:
            x, h, gates = mm_res([xb], w_ple_gate[i].astype(BF16), x, row(g_post_ple[i]), second="norm",
                                 g2=row(g_pre_mix[i + 1]), wif=w_ifs[i + 1], bif=b_ifs[i + 1], n_heads=n_heads,
                                 **ple_args)
        else:
            (x,) = mm_res([xb], w_ple_gate[i].astype(BF16), x, row(g_post_ple[i]), **ple_args)

        first = CACHE_PAD - (width - 1)
        outs["pc"].append(c1); outs["pn"].append(n1); outs["pm"].append(m1[:, 0, :n_heads])
        outs["pre"].append(re_p); outs["pim"].append(im_p); outs["pcv"].append(cv_p[:, first:])
        outs["sc"].append(c2); outs["sn"].append(n2); outs["sm"].append(m2[:, 0, :n_heads])
        outs["sre"].append(re_s); outs["sim"].append(im_s); outs["scv"].append(cv_s[:, first:])

    st = lambda k: jnp.stack(outs[k])
    return (x[:np_rows].reshape(bp, lp, d_model), x[np_rows:].reshape(bs, ls, d_model),
            st("pc"), st("pn"), st("pm"), st("pre"), st("pim"), st("pcv"),
            st("sc"), st("sn"), st("sm"), st("sre"), st("sim"), st("scv"))
```

```python
import functools
import math

import jax
import jax.numpy as jnp
from jax import lax
from jax.experimental import pallas as pl
from jax.experimental.pallas import tpu as pltpu

F32 = jnp.float32
BF16 = jnp.bfloat16
EPS = 1e-6
LANES = 128
SUBLANES = 8
VMEM_LIMIT = 56 * 1024 * 1024
GATE_PAD = LANES
CACHE_PAD = 32


def _pick(n, pref):
    t = min(pref, n)
    while n % t:
        t //= 2
    return t


def _params(*sem):
    return pltpu.CompilerParams(dimension_semantics=sem, vmem_limit_bytes=VMEM_LIMIT)


def _rms(x, g):
    return x * lax.rsqrt(jnp.mean(x * x, axis=-1, keepdims=True) + EPS) * g


def _log_sigmoid(x):
    return jnp.minimum(x, 0.0) - jnp.log1p(jnp.exp(-jnp.abs(x)))


def _dot(a, b):
    return jnp.dot(a, b, preferred_element_type=F32)


def _gate_act(gt, n_heads):
    lane = lax.broadcasted_iota(jnp.int32, gt.shape, 1)
    return jnp.where(lane >= n_heads, _log_sigmoid(gt), gt)


def _rms_gates_kernel(x_ref, g_ref, wif_ref, bias_ref, *rest, n_heads):
    h_ref, gate_ref = rest[-2:]
    hb = _rms(x_ref[...], g_ref[...]).astype(BF16)
    h_ref[...] = hb
    gate_ref[...] = _gate_act(_dot(hb, wif_ref[...]) + bias_ref[...], n_heads)


def rms_gates(x, g, wif, bias, n_heads, prev, row_off, n_rows):
    n, d = x.shape
    tm = _pick(n, 256)
    r0 = row_off // tm
    alias = [] if prev is None else list(prev)
    return pl.pallas_call(
        functools.partial(_rms_gates_kernel, n_heads=n_heads),
        grid=(n // tm,),
        input_output_aliases={4: 0, 5: 1} if alias else {},
        in_specs=[pl.BlockSpec((tm, d), lambda i: (i, 0)),
                  pl.BlockSpec((1, d), lambda i: (0, 0)),
                  pl.BlockSpec((d, GATE_PAD), lambda i: (0, 0)),
                  pl.BlockSpec((1, GATE_PAD), lambda i: (0, 0))]
        + [pl.BlockSpec(memory_space=pl.ANY)] * len(alias),
        out_specs=[pl.BlockSpec((tm, d), lambda i: (r0 + i, 0)),
                   pl.BlockSpec((tm, GATE_PAD), lambda i: (r0 + i, 0))],
        out_shape=[jax.ShapeDtypeStruct((n_rows, d), BF16),
                   jax.ShapeDtypeStruct((n_rows, GATE_PAD), F32)],
        compiler_params=_params("parallel"), name="rms_gates",
    )(x, g, wif, bias, *alias)


def _mm_kernel(a_ref, b_ref, o_ref):
    o_ref[...] = _dot(a_ref[...], b_ref[...]).astype(o_ref.dtype)


def mm(a, b, out_dtype):
    m, k = a.shape
    n = b.shape[1]
    tm, tn = _pick(m, 1024), _pick(n, 1024)
    return pl.pallas_call(
        _mm_kernel,
        grid=(m // tm, n // tn),
        in_specs=[pl.BlockSpec((tm, k), lambda i, j: (i, 0)),
                  pl.BlockSpec((k, tn), lambda i, j: (0, j))],
        out_specs=pl.BlockSpec((tm, tn), lambda i, j: (i, j)),
        out_shape=jax.ShapeDtypeStruct((m, n), out_dtype),
        compiler_params=_params("parallel", "arbitrary"), name="mm_in",
    )(a, b)


def _mm_swiglu_kernel(a_ref, bg_ref, bu_ref, o_ref, *, d_ff):
    a = a_ref[...]
    gate = _dot(a, bg_ref[...])
    up = _dot(a, bu_ref[...])
    f = gate * jax.nn.sigmoid(gate) * up
    col = pl.program_id(1) * f.shape[1] + lax.broadcasted_iota(jnp.int32, f.shape, 1)
    o_ref[...] = jnp.where(col < d_ff, f, 0.0).astype(o_ref.dtype)


def mm_swiglu(a, bg, bu, n):
    m, k = a.shape
    d_ff = bg.shape[1]
    tm, tn = _pick(m, 1024), _pick(n, 512)
    wspec = pl.BlockSpec((k, tn), lambda i, j: (0, jnp.minimum(j, (d_ff - 1) // tn)))
    return pl.pallas_call(
        functools.partial(_mm_swiglu_kernel, d_ff=d_ff),
        grid=(m // tm, n // tn),
        in_specs=[pl.BlockSpec((tm, k), lambda i, j: (i, 0)), wspec, wspec],
        out_specs=pl.BlockSpec((tm, tn), lambda i, j: (i, j)),
        out_shape=jax.ShapeDtypeStruct((m, n), BF16),
        compiler_params=_params("parallel", "arbitrary"), name="mm_swiglu",
    )(a, bg, bu)


EPI_CHUNKS = 8
MXU_DEPTH = 256


def _mm_res_kernel(*refs, n_a, split, n_tiles, n_steps, tn, rb, ple, second, gates, n_heads, n_x, n_xo, p_chunks):
    refs = list(refs)
    a_refs = [refs.pop(0) for _ in range(n_a)]
    b_ref = refs.pop(0)
    x_refs = [refs.pop(0) for _ in range(n_x)]
    g_ref = refs.pop(0)
    g2_ref = refs.pop(0) if second == "norm" else None
    pe_ref, wple_ref = (refs.pop(0), refs.pop(0)) if ple else (None, None)
    wif_ref, bif_ref = (refs.pop(0), refs.pop(0)) if gates else (None, None)
    xo_refs = [refs.pop(0) for _ in range(n_xo)]
    h_ref = refs.pop(0) if second else None
    gate_ref = refs.pop(0) if gates else None
    acc_refs = (refs.pop(0), refs.pop(0))
    i, s = pl.program_id(0), pl.program_id(1)
    in_first = jnp.where(i == 0, 0, (i - 1) * EPI_CHUNKS + jnp.minimum(s, EPI_CHUNKS - 1)) < p_chunks

    def matmul(acc_ref):
        if split == "n":
            part, k0 = None, 0
            for a_ref in a_refs:
                kw = a_ref.shape[1]
                d = _dot(a_ref[...], b_ref[k0:k0 + kw, :])
                part = d if part is None else part + d
                k0 += kw
            if ple:
                part = _dot(pe_ref[...], wple_ref[...]) * jax.nn.sigmoid(part)
            acc_ref[:, pl.ds(pl.multiple_of(s * tn, tn), tn)] = part
        else:
            acc_ref[...] += _dot(a_refs[0][...], b_ref[...])

    def epilogue(acc_ref):
        rows = pl.ds(pl.multiple_of(s * rb, rb), rb)
        f = acc_ref[rows, :]
        if split == "k":
            acc_ref[rows, :] = jnp.zeros_like(f)
        x = x_refs[0][...] if n_x == 1 else jnp.where(in_first, x_refs[0][...], x_refs[1][...])
        xn = x + _rms(f, g_ref[...])
        if n_xo == 1:
            xo_refs[0][...] = xn
        else:
            xo_refs[0][...] = jnp.where(in_first, xn, xo_refs[0][...])
            xo_refs[1][...] = jnp.where(in_first, xo_refs[1][...], xn)
        if second == "norm":
            hb = _rms(xn, g2_ref[...]).astype(BF16)
            h_ref[...] = hb
            if gates:
                gate_ref[...] = _gate_act(_dot(hb, wif_ref[...]) + bif_ref[...], n_heads)
        elif second == "cast":
            h_ref[...] = xn.astype(BF16)

    @pl.when((i == 0) & (s == 0))
    def _():
        for ref in list(acc_refs) + (xo_refs if n_xo == 2 else []):
            ref[...] = jnp.zeros_like(ref)

    for parity in range(2):
        mine = (i < n_tiles) & (lax.rem(i, 2) == parity)

        @pl.when(mine & (s < EPI_CHUNKS))
        def _(parity=parity):
            epilogue(acc_refs[1 - parity])
            matmul(acc_refs[parity])

        if n_steps > EPI_CHUNKS:
            @pl.when(mine & (s >= EPI_CHUNKS))
            def _(parity=parity):
                matmul(acc_refs[parity])

    @pl.when((i == n_tiles) & (s < EPI_CHUNKS))
    def _():
        epilogue(acc_refs[1 - n_tiles % 2])


def mm_res(a_parts, b, x, g, *, split, second=None, g2=None, pe=None, wple=None, wif=None, bif=None,
           n_heads=0, tm_pref=512, split_out=None, name="mm_res"):
    xs = list(x) if isinstance(x, (tuple, list)) else [x]
    m = sum(v.shape[0] for v in xs)
    kdim, n = b.shape
    tm = _pick(m, tm_pref)
    n_tiles, nch = m // tm, EPI_CHUNKS
    rb = tm // nch
    ple, gates = pe is not None, wif is not None
    once = pl.Buffered(1)
    const = lambda shape: pl.BlockSpec(shape, lambda i, s: (0, 0), pipeline_mode=once)
    tile = lambda i: jnp.minimum(i, n_tiles - 1)
    chunk = lambda i, s: jnp.where(i == 0, 0, (i - 1) * nch + jnp.minimum(s, nch - 1))
    chunk_spec = lambda w: pl.BlockSpec((rb, w), lambda i, s: (chunk(i, s), 0))
    first_rows = xs[0].shape[0] if len(xs) == 2 else (split_out or 0)
    assert first_rows % rb == 0 and (len(xs) == 1 or not split_out or split_out == xs[0].shape[0])
    p_chunks = first_rows // rb
    group_specs = [pl.BlockSpec((rb, n), lambda i, s: (jnp.minimum(chunk(i, s), p_chunks - 1), 0)),
                   pl.BlockSpec((rb, n), lambda i, s: (jnp.maximum(chunk(i, s) - p_chunks, 0), 0))]
    if split == "n":
        n_steps = nch
        tn = n // n_steps
        in_specs = [pl.BlockSpec((tm, a.shape[1]), lambda i, s: (tile(i), 0)) for a in a_parts]
        in_specs.append(pl.BlockSpec((kdim, tn), lambda i, s: (0, s)))
    else:
        (a,) = a_parts
        tn = n
        tk = max(t for t in range(MXU_DEPTH, kdim // nch + 1, MXU_DEPTH) if kdim % t == 0)
        n_steps = kdim // tk
        in_specs = [pl.BlockSpec((tm, tk), lambda i, s: (tile(i), s)),
                    pl.BlockSpec((tk, n), lambda i, s: (s, 0))]
    in_specs += (group_specs if len(xs) == 2 else [chunk_spec(n)]) + [const((1, n))]
    args = list(a_parts) + [b] + xs + [g]
    if second == "norm":
        in_specs.append(const((1, n)))
        args.append(g2)
    if ple:
        assert split == "n"
        in_specs += [pl.BlockSpec((tm, pe.shape[1]), lambda i, s: (tile(i), 0)),
                     pl.BlockSpec((wple.shape[0], tn), lambda i, s: (0, s))]
        args += [pe, wple]
    if gates:
        in_specs += [const(wif.shape), const(bif.shape)]
        args += [wif, bif]
    if split_out:
        out_specs = list(group_specs)
        out_shape = [jax.ShapeDtypeStruct((split_out, n), F32), jax.ShapeDtypeStruct((m - split_out, n), F32)]
    else:
        out_specs, out_shape = [chunk_spec(n)], [jax.ShapeDtypeStruct((m, n), F32)]
    if second:
        out_specs.append(chunk_spec(n))
        out_shape.append(jax.ShapeDtypeStruct((m, n), BF16))
    if gates:
        out_specs.append(chunk_spec(GATE_PAD))
        out_shape.append(jax.ShapeDtypeStruct((m, GATE_PAD), F32))
    return pl.pallas_call(
        functools.partial(_mm_res_kernel, n_a=len(a_parts), split=split, n_tiles=n_tiles, n_steps=n_steps, tn=tn,
                          rb=rb, ple=ple,
                          second=second, gates=gates, n_heads=n_heads, n_x=len(xs), n_xo=2 if split_out else 1,
                          p_chunks=p_chunks),
        grid=(n_tiles + 1, n_steps),
        in_specs=in_specs, out_specs=out_specs, out_shape=out_shape,
        scratch_shapes=[pltpu.VMEM((tm, n), F32), pltpu.VMEM((tm, n), F32)],
        compiler_params=_params("arbitrary", "arbitrary"), name=name,
    )(*args)


def _split3(x):
    hi = x.astype(BF16)
    r1 = x - hi.astype(F32)
    mid = r1.astype(BF16)
    lo = (r1 - mid.astype(F32)).astype(BF16)
    return hi, mid, lo


def _mlstm_kernel(zq_ref, zk_ref, zv_ref, zo_ref, gate_ref, c0_ref, n0_ref, m0_ref,
                  gmh_ref, *rest, n_heads, dh, t):
    y_ref, c_ref, n_ref, m_ref = rest[-4:]

    @pl.when(pl.program_id(1) == 0)
    def _():
        c_ref[...] = c0_ref[...]
        n_ref[...] = n0_ref[...]
        m_ref[...] = m0_ref[...]

    row = lax.broadcasted_iota(jnp.int32, (t, t), 0)
    col = lax.broadcasted_iota(jnp.int32, (t, t), 1)
    causal = row >= col
    tri = causal.astype(BF16)
    tri_t = (col >= row).astype(BF16)
    g_col = gate_ref[...]
    g_row = g_col.T
    cum_col = sum(_dot(tri, p) for p in _split3(g_col))
    cum_row = sum(_dot(p, tri_t) for p in _split3(g_row))
    scale = dh ** -0.5

    for h in range(n_heads):
        sl = slice(h * dh, (h + 1) * dh)
        q = zq_ref[:, sl]
        k = zk_ref[:, sl]
        v = zv_ref[:, sl]
        i_col = g_col[:, h:h + 1]
        i_row = g_row[h:h + 1, :]
        b_col = cum_col[:, n_heads + h:n_heads + h + 1]
        b_row = cum_row[n_heads + h:n_heads + h + 1, :]
        m_prev = m_ref[0, :, h:h + 1]
        c_prev = c_ref[0, h]
        n_prev = n_ref[0, h:h + 1, :]

        d = jnp.where(causal, b_col - b_row + i_row, -jnp.inf)
        inter = b_col + m_prev
        mt = jnp.maximum(inter, jnp.max(d, axis=1, keepdims=True))
        s = lax.dot_general(q, k, (((1,), (1,)), ((), ())), preferred_element_type=F32)
        s = s * scale * jnp.exp(d - mt)
        sc_t = jnp.exp(inter - mt)
        num = sc_t * _dot(q, c_prev.astype(BF16)) + _dot(s.astype(BF16), v)
        qn = jnp.sum(q.astype(F32) * n_prev, axis=1, keepdims=True)
        den = sc_t * qn + jnp.sum(s, axis=1, keepdims=True)
        hh = num / jnp.maximum(jnp.abs(den), jnp.exp(-mt))

        b_last = b_col[t - 1:t, :]
        m_new = mt[t - 1:t, :]
        wl = jnp.exp(b_last - b_col + i_col - m_new)
        sc = jnp.exp(b_last + m_prev - m_new)
        kw = k.astype(F32) * wl
        kv = lax.dot_general(kw.astype(BF16), v, (((0,), (0,)), ((), ())), preferred_element_type=F32)
        c_ref[0, h] = sc * c_prev + kv * scale
        n_ref[0, h:h + 1, :] = sc * n_prev + jnp.sum(kw, axis=0, keepdims=True) * scale
        m_ref[0, :, h:h + 1] = m_new

        hn = _rms(hh, gmh_ref[:, sl])
        y_ref[:, sl] = (hn * jax.nn.sigmoid(zo_ref[:, sl].astype(F32))).astype(BF16)


def mlstm(z, gates, c0, n0, m0, gmh, y_prev, row_off, bsz, seq, n_heads, dh):
    d_a = n_heads * dh
    n_rows = z.shape[0]
    t = min(256, seq)
    nc = seq // t
    r0 = row_off // t
    rmap = lambda b, c: r0 + b * nc + c
    zspec = lambda j: pl.BlockSpec((t, d_a), lambda b, c, j=j: (rmap(b, c), j))
    alias = [] if y_prev is None else [y_prev]
    return pl.pallas_call(
        functools.partial(_mlstm_kernel, n_heads=n_heads, dh=dh, t=t),
        grid=(bsz, nc),
        input_output_aliases={9: 0} if alias else {},
        in_specs=[zspec(0), zspec(1), zspec(2), zspec(3),
                  pl.BlockSpec((t, GATE_PAD), lambda b, c: (rmap(b, c), 0)),
                  pl.BlockSpec((1, n_heads, dh, dh), lambda b, c: (b, 0, 0, 0)),
                  pl.BlockSpec((1, n_heads, dh), lambda b, c: (b, 0, 0)),
                  pl.BlockSpec((1, 1, LANES), lambda b, c: (b, 0, 0)),
                  pl.BlockSpec((1, d_a), lambda b, c: (0, 0))]
        + [pl.BlockSpec(memory_space=pl.ANY)] * len(alias),
        out_specs=[pl.BlockSpec((t, d_a), lambda b, c: (rmap(b, c), 0)),
                   pl.BlockSpec((1, n_heads, dh, dh), lambda b, c: (b, 0, 0, 0)),
                   pl.BlockSpec((1, n_heads, dh), lambda b, c: (b, 0, 0)),
                   pl.BlockSpec((1, 1, LANES), lambda b, c: (b, 0, 0))],
        out_shape=[jax.ShapeDtypeStruct((n_rows, d_a), BF16),
                   jax.ShapeDtypeStruct((bsz, n_heads, dh, dh), F32),
                   jax.ShapeDtypeStruct((bsz, n_heads, dh), F32),
                   jax.ShapeDtypeStruct((bsz, 1, LANES), F32)],
        compiler_params=_params("parallel", "arbitrary"), name="mlstm",
    )(z, z, z, z, gates, c0, n0, m0, gmh, *alias)


S5_ROWS = 512


def _s5_disc_kernel(lr_ref, li_ref, ldt_ref, br_ref, bi_ref, abr_ref, abi_ref, bbr_ref, bbi_ref):
    dt = jnp.exp(ldt_ref[...])
    lr = lr_ref[...]
    li = li_ref[...]
    mag = jnp.exp(lr * dt)
    ab_re = mag * jnp.cos(li * dt)
    ab_im = mag * jnp.sin(li * dt)
    den = lr * lr + li * li
    nr = ab_re - 1.0
    coef_re = (nr * lr + ab_im * li) / den
    coef_im = (ab_im * lr - nr * li) / den
    br = br_ref[...]
    bi = bi_ref[...]
    abr_ref[...] = ab_re
    abi_ref[...] = ab_im
    bbr_ref[...] = coef_re * br - coef_im * bi
    bbi_ref[...] = coef_re * bi + coef_im * br


def s5_discretise(lam_re, lam_im, log_dt, b_re, b_im):
    g, p, j = b_re.shape
    rep = lambda a: jnp.repeat(a, j, axis=0)
    br_t = jnp.swapaxes(b_re, 1, 2).reshape(g * j, p)
    bi_t = jnp.swapaxes(b_im, 1, 2).reshape(g * j, p)
    shp = jax.ShapeDtypeStruct((g * j, p), F32)
    abr, abi, bbr, bbi = pl.pallas_call(
        _s5_disc_kernel, out_shape=[shp, shp, shp, shp], name="s5_disc",
    )(rep(lam_re), rep(lam_im), rep(log_dt[:, None]), br_t, bi_t)
    first = lambda a: a.reshape(g, j, p)[:, 0]
    return first(abr), first(abi), bbr.reshape(g, j, p), bbi.reshape(g, j, p)


def _s5_scan_kernel(*refs, s_rows, tc, nq, gw, slab, emit_y):
    if emit_y:
        u_ref, h0_ref, a_ref, bblk_ref, cblk_ref, d_ref, wglu_ref, y_ref, hout_ref, bu_scr = refs
    else:
        u_ref, h0_ref, a_ref, bblk_ref, hout_ref, bu_scr = refs

    @pl.when(pl.program_id(0) == 0)
    def _():
        hout_ref[...] = h0_ref[...]

    u = u_ref[...]
    kq = u.shape[1] // nq
    for q in range(nq):
        bu_scr[:, q * 2 * gw:(q + 1) * 2 * gw] = _dot(u[:, q * kq:(q + 1) * kq], bblk_ref[q])

    for q in range(nq):
        for lo in range(0, gw, slab):
            re = slice(q * 2 * gw + lo, q * 2 * gw + lo + slab)
            im = slice(q * 2 * gw + gw + lo, q * 2 * gw + gw + lo + slab)
            ar = a_ref[:, re]
            ai = a_ref[:, im]

            def body(step, carry, re=re, im=im, ar=ar, ai=ai):
                hr, hi = carry
                rows = pl.ds(pl.multiple_of(step * s_rows, s_rows), s_rows)
                nr = ar * hr - ai * hi + bu_scr[rows, re]
                ni = ar * hi + ai * hr + bu_scr[rows, im]
                bu_scr[rows, re] = nr
                bu_scr[rows, im] = ni
                return nr, ni

            hr, hi = lax.fori_loop(0, tc, body, (hout_ref[:, re], hout_ref[:, im]))
            hout_ref[:, re] = hr
            hout_ref[:, im] = hi

    if emit_y:
        ys = [_dot(bu_scr[:, q * 2 * gw:(q + 1) * 2 * gw].astype(BF16), cblk_ref[q]) for q in range(nq)]
        y = jnp.concatenate(ys, axis=1) + d_ref[...] * u.astype(F32)
        zg = 0.5 * y * (1.0 + jnp.tanh(math.sqrt(2.0 / math.pi) * (y + 0.044715 * (y * y * y))))
        y_ref[...] = (zg * jax.nn.sigmoid(_dot(zg.astype(BF16), wglu_ref[...]))).astype(BF16)


def s5_scan(u_perm, h0, a_blk, bblk, cblk, d, wglu, s_rows, emit_y):
    rows, d_b = u_perm.shape
    nq, kq, w2 = bblk.shape
    gw = w2 // 2
    steps = rows // s_rows
    tc = _pick(steps, max(1, S5_ROWS // s_rows))
    rb = tc * s_rows
    wtot = nq * w2
    const2 = lambda c: (0, 0)
    in_specs = [pl.BlockSpec((rb, d_b), lambda c: (c, 0)),
                pl.BlockSpec((s_rows, wtot), const2),
                pl.BlockSpec((1, wtot), const2),
                pl.BlockSpec((nq, kq, w2), lambda c: (0, 0, 0))]
    args = [u_perm, h0, a_blk, bblk]
    out_specs = [pl.BlockSpec((s_rows, wtot), const2)]
    out_shape = [jax.ShapeDtypeStruct((s_rows, wtot), F32)]
    if emit_y:
        in_specs += [pl.BlockSpec((nq, w2, kq), lambda c: (0, 0, 0)),
                     pl.BlockSpec((1, d_b), const2),
                     pl.BlockSpec((d_b, d_b), const2)]
        args += [cblk, d, wglu]
        out_specs = [pl.BlockSpec((rb, d_b), lambda c: (c, 0))] + out_specs
        out_shape = [jax.ShapeDtypeStruct((rows, d_b), BF16)] + out_shape
    return pl.pallas_call(
        functools.partial(_s5_scan_kernel, s_rows=s_rows, tc=tc, nq=nq, gw=gw,
                          slab=min(512, gw), emit_y=emit_y),
        grid=(steps // tc,),
        in_specs=in_specs, out_specs=out_specs, out_shape=out_shape,
        scratch_shapes=[pltpu.VMEM((rb, wtot), F32)],
        compiler_params=_params("arbitrary"), name="s5_scan" if emit_y else "s5_ends",
    )(*args)


def _s5_chain_kernel(sfin_ref, a_ref, hin_ref, *, seg_len, n_seg, nq, gw):
    for q in range(nq):
        re = slice(q * 2 * gw, q * 2 * gw + gw)
        im = slice(q * 2 * gw + gw, (q + 1) * 2 * gw)
        br, bi = a_ref[:, re], a_ref[:, im]
        pr, pi = jnp.ones_like(br), jnp.zeros_like(br)
        e = seg_len
        while e:
            if e & 1:
                pr, pi = pr * br - pi * bi, pr * bi + pi * br
            br, bi = br * br - bi * bi, 2.0 * br * bi
            e >>= 1
        hr, hi = jnp.zeros_like(pr), jnp.zeros_like(pr)
        for s in range(n_seg):
            hin_ref[s:s + 1, re] = hr
            hin_ref[s:s + 1, im] = hi
            hr, hi = (pr * hr - pi * hi + sfin_ref[s:s + 1, re],
                      pr * hi + pi * hr + sfin_ref[s:s + 1, im])


def s5_chain(sfin, a_blk, seg_len, nq):
    n_seg, wtot = sfin.shape
    return pl.pallas_call(
        functools.partial(_s5_chain_kernel, seg_len=seg_len, n_seg=n_seg, nq=nq, gw=wtot // nq // 2),
        out_shape=jax.ShapeDtypeStruct((n_seg, wtot), F32), name="s5_chain",
    )(sfin, a_blk)


def _conv_kernel(val_ref, gate_ref, cache_ref, w_ref, b_ref, lng_ref, lnb_ref, *rest, t, width):
    y_ref, cout_ref, xp_scr, xs_scr = rest[-4:]
    c = pl.program_id(1)

    @pl.when(c == 0)
    def _():
        xp_scr[0:CACHE_PAD, :] = cache_ref[0]

    @pl.when(c > 0)
    def _():
        xp_scr[0:CACHE_PAD, :] = xp_scr[t:t + CACHE_PAD, :]

    xp_scr[CACHE_PAD:CACHE_PAD + t, :] = val_ref[...].astype(F32) * jax.nn.sigmoid(gate_ref[...].astype(F32))
    first = CACHE_PAD - (width - 1)
    acc = None
    for r in range(min(SUBLANES, width)):
        taps = range(r, width, SUBLANES)
        rows = t + SUBLANES * (len(taps) - 1)
        xs_scr[0:rows, :] = xp_scr[first + r:first + r + rows, :]
        for a, j in enumerate(taps):
            term = w_ref[j:j + 1, :] * xs_scr[SUBLANES * a:SUBLANES * a + t, :]
            acc = term if acc is None else acc + term
    y = acc + b_ref[...]
    mu = jnp.mean(y, axis=-1, keepdims=True)
    yc = y - mu
    var = jnp.mean(yc * yc, axis=-1, keepdims=True)
    y = yc * lax.rsqrt(var + EPS) * lng_ref[...] + lnb_ref[...]
    y_ref[...] = (y * jax.nn.sigmoid(y)).astype(BF16)
    cout_ref[0] = xp_scr[t:t + CACHE_PAD, :]


def conv_module(z, cache, w, b, lng, lnb, y_prev, row_off, bsz, seq, col_val, col_gate):
    width, d_c = w.shape
    alias = [] if y_prev is None else [y_prev]
    t = min(256, seq)
    assert t >= CACHE_PAD
    nc = seq // t
    r0 = row_off // t
    rmap = lambda bb, c: r0 + bb * nc + c
    vec = pl.BlockSpec((1, d_c), lambda bb, c: (0, 0))
    return pl.pallas_call(
        functools.partial(_conv_kernel, t=t, width=width),
        grid=(bsz, nc),
        input_output_aliases={7: 0} if alias else {},
        in_specs=[pl.BlockSpec((t, d_c), lambda bb, c: (rmap(bb, c), col_val)),
                  pl.BlockSpec((t, d_c), lambda bb, c: (rmap(bb, c), col_gate)),
                  pl.BlockSpec((1, CACHE_PAD, d_c), lambda bb, c: (bb, 0, 0)),
                  pl.BlockSpec((width, d_c), lambda bb, c: (0, 0)),
                  vec, vec, vec] + [pl.BlockSpec(memory_space=pl.ANY)] * len(alias),
        out_specs=[pl.BlockSpec((t, d_c), lambda bb, c: (rmap(bb, c), 0)),
                   pl.BlockSpec((1, CACHE_PAD, d_c), lambda bb, c: (bb, 0, 0))],
        out_shape=[jax.ShapeDtypeStruct((z.shape[0], d_c), BF16),
                   jax.ShapeDtypeStruct((bsz, CACHE_PAD, d_c), F32)],
        scratch_shapes=[pltpu.VMEM((CACHE_PAD + t, d_c), F32), pltpu.VMEM((CACHE_PAD + t, d_c), F32)],
        compiler_params=_params("parallel", "arbitrary"), name="conv",
    )(z, z, cache, w, b, lng, lnb, *alias)


def _block_diag(w, gq):
    g, r, c = w.shape
    eye = jnp.eye(gq, dtype=w.dtype)
    w = w.reshape(g // gq, gq, r, c)
    return jnp.einsum("qgrc,gh->qgrhc", w, eye).reshape(g // gq, gq * r, gq * c)


def _state_to_lanes(re, im, gq):
    s, g, p = re.shape
    both = jnp.stack([re.reshape(s, g // gq, gq * p), im.reshape(s, g // gq, gq * p)], axis=2)
    return both.reshape(s, -1)


def _lanes_to_state(h, g, p, gq):
    s = h.shape[0]
    both = h.reshape(s, g // gq, 2, gq * p)
    return both[:, :, 0].reshape(s, g, p), both[:, :, 1].reshape(s, g, p)


def _to_step_major(x, n_seq):
    rows, d = x.shape
    return x.reshape(n_seq, rows // n_seq, d).swapaxes(0, 1).reshape(rows, d)


def _from_step_major(x, n_seq):
    rows, d = x.shape
    return x.reshape(rows // n_seq, n_seq, d).swapaxes(0, 1).reshape(rows, d)


PROMPT_SEGMENTS = 32


def kernel(x_prompt, x_sample, p_prompt, p_sample, state_mlstm_c, state_mlstm_n, state_mlstm_m, state_s5_re, state_s5_im, cache_conv, g_pre_mix, w_in, b_igate, b_fgate, g_mh, s5_lam_re, s5_lam_im, s5_log_dt, s5_b_re, s5_b_im, s5_c_re, s5_c_im, s5_d, s5_w_glu, conv_w, conv_b, conv_ln_g, conv_ln_b, w_out, g_post_mix, g_pre_ffn, w_ffn_gate, w_ffn_up, w_ffn_down, g_post_ffn, w_ple, w_ple_gate, g_post_ple):
    depth = w_in.shape[0]
    bp, lp, d_model = x_prompt.shape
    bs, ls, _ = x_sample.shape
    n_heads = b_igate.shape[1]
    d_a = g_mh.shape[1]
    dh = d_a // n_heads
    g_b, p_b, j_b = s5_b_re.shape[1:]
    d_b = g_b * j_b
    width, d_c = conv_w.shape[1:]
    d_ff = w_ffn_gate.shape[2]
    assert bp == 1 and 2 * n_heads <= GATE_PAD and d_b == d_c and d_a % d_b == 0
    gq = 256 // j_b
    nq = g_b // gq
    np_rows, ns_rows = bp * lp, bs * ls
    n_seg = min(PROMPT_SEGMENTS, lp // 8)
    seg_len = lp // n_seg
    col_u = 4 * d_a // d_b
    col_val, col_gate = col_u + 1, col_u + 2
    d_ff_pad = -(-d_ff // 1024) * 1024

    n_rows = np_rows + ns_rows
    x = (x_prompt.reshape(np_rows, d_model), x_sample.reshape(ns_rows, d_model))
    row = lambda v: v.reshape(1, -1).astype(F32)
    zeros = lambda *s: jnp.zeros(s, F32)
    pad_m = lambda m: jnp.pad(m, ((0, 0), (0, LANES - n_heads)))[:, None, :]
    pad_cache = lambda c: jnp.pad(c, ((0, 0), (CACHE_PAD - (width - 1), 0), (0, 0)))

    n_qkvo = 4 * d_a
    w_ifs = [jnp.pad(w_in[i][:, n_qkvo:n_qkvo + 2 * n_heads], ((0, 0), (0, GATE_PAD - 2 * n_heads))).astype(BF16)
             for i in range(depth)]
    b_ifs = [jnp.pad(jnp.concatenate([b_igate[i], b_fgate[i]]), (0, GATE_PAD - 2 * n_heads)).reshape(1, GATE_PAD)
             for i in range(depth)]
    outs = {k: [] for k in ("pc", "pn", "pm", "pre", "pim", "pcv", "sc", "sn", "sm", "sre", "sim", "scv")}
    for i in range(depth):
        w_main = jnp.concatenate([w_in[i][:, :n_qkvo], w_in[i][:, n_qkvo + 2 * n_heads:]], axis=1).astype(BF16)
        w_gate = w_ffn_gate[i].astype(BF16)
        w_up = w_ffn_up[i].astype(BF16)
        w_down = jnp.pad(w_ffn_down[i], ((0, d_ff_pad - d_ff), (0, 0))).astype(BF16)

        ab_re, ab_im, bb_re, bb_im = s5_discretise(s5_lam_re[i], s5_lam_im[i], s5_log_dt[i], s5_b_re[i], s5_b_im[i])
        bblk = jnp.concatenate([_block_diag(bb_re, gq), _block_diag(bb_im, gq)], axis=2).astype(BF16)
        c_re_t = jnp.swapaxes(s5_c_re[i], 1, 2)
        c_im_t = jnp.swapaxes(s5_c_im[i], 1, 2)
        cblk = jnp.concatenate([_block_diag(c_re_t, gq), _block_diag(-c_im_t, gq)], axis=1).astype(BF16)
        a_blk = _state_to_lanes(ab_re[None], ab_im[None], gq)
        wglu = s5_w_glu[i].astype(BF16)
        d_row = row(s5_d[i])

        if i == 0:
            hg = rms_gates(x[0], row(g_pre_mix[i]), w_ifs[i], b_ifs[i], n_heads, None, 0, n_rows)
            h, gates = rms_gates(x[1], row(g_pre_mix[i]), w_ifs[i], b_ifs[i], n_heads, hg, np_rows, n_rows)
        z = mm(h, w_main, BF16)

        y_a, c1, n1, m1 = mlstm(z, gates, zeros(bp, n_heads, dh, dh), zeros(bp, n_heads, dh),
                                zeros(bp, 1, LANES), row(g_mh[i]), None, 0, bp, lp, n_heads, dh)
        y_a, c2, n2, m2 = mlstm(z, gates, state_mlstm_c[i], state_mlstm_n[i], pad_m(state_mlstm_m[i]),
                                row(g_mh[i]), y_a, np_rows, bs, ls, n_heads, dh)

        u_all = z[:, 4 * d_a:4 * d_a + d_b]
        u_p = _to_step_major(u_all[:np_rows], n_seg)
        u_s = _to_step_major(u_all[np_rows:], bs)
        wtot = a_blk.shape[1]
        (sfin,) = s5_scan(u_p, zeros(n_seg, wtot), a_blk, bblk, None, None, None, n_seg, False)
        hin = s5_chain(sfin, a_blk, seg_len, nq)
        yb_p, hfin_p = s5_scan(u_p, hin, a_blk, bblk, cblk, d_row, wglu, n_seg, True)
        yb_s, hfin_s = s5_scan(u_s, _state_to_lanes(state_s5_re[i], state_s5_im[i], gq), a_blk, bblk, cblk,
                               d_row, wglu, bs, True)
        y_b = jnp.concatenate([_from_step_major(yb_p, n_seg), _from_step_major(yb_s, bs)], axis=0)
        re_p, im_p = _lanes_to_state(hfin_p[n_seg - 1:], g_b, p_b, gq)
        re_s, im_s = _lanes_to_state(hfin_s, g_b, p_b, gq)

        cw, cb, lg, lb = conv_w[i], row(conv_b[i]), row(conv_ln_g[i]), row(conv_ln_b[i])
        y_c, cv_p = conv_module(z, zeros(bp, CACHE_PAD, d_c), cw, cb, lg, lb, None, 0, bp, lp, col_val, col_gate)
        y_c, cv_s = conv_module(z, pad_cache(cache_conv[i]), cw, cb, lg, lb, y_c, np_rows, bs, ls, col_val, col_gate)

        x, h = mm_res([y_a, y_b, y_c], w_out[i].astype(BF16), x, row(g_post_mix[i]), split="n",
                      second="norm", g2=row(g_pre_ffn[i]), name="mm_out")

        f = mm_swiglu(h, w_gate, w_up, d_ff_pad)
        x, xb = mm_res([f], w_down, x, row(g_post_ffn[i]), split="k", second="cast",
                       name="mm_down")

        pe = jnp.concatenate([p_prompt[i].reshape(np_rows, -1), p_sample[i].reshape(ns_rows, -1)], axis=0).astype(BF16)
        ple_args = dict(split="n", pe=pe, wple=w_ple[i].astype(BF16), name="mm_ple")
        if i + 1 < depth:
            x, h, gates = mm_res([xb], w_ple_gate[i].astype(BF16), x, row(g_post_ple[i]), second="norm",
                                 g2=row(g_pre_mix[i + 1]), wif=w_ifs[i + 1], bif=b_ifs[i + 1], n_heads=n_heads,
                                 **ple_args)
        else:
            x = mm_res([xb], w_ple_gate[i].astype(BF16), x, row(g_post_ple[i]), split_out=np_rows, **ple_args)

        first = CACHE_PAD - (width - 1)
        outs["pc"].append(c1); outs["pn"].append(n1); outs["pm"].append(m1[:, 0, :n_heads])
        outs["pre"].append(re_p); outs["pim"].append(im_p); outs["pcv"].append(cv_p[:, first:])
        outs["sc"].append(c2); outs["sn"].append(n2); outs["sm"].append(m2[:, 0, :n_heads])
        outs["sre"].append(re_s); outs["sim"].append(im_s); outs["scv"].append(cv_s[:, first:])

    st = lambda k: jnp.stack(outs[k])
    return (x[0].reshape(bp, lp, d_model), x[1].reshape(bs, ls, d_model),
            st("pc"), st("pn"), st("pm"), st("pre"), st("pim"), st("pcv"),
            st("sc"), st("sn"), st("sm"), st("sre"), st("sim"), st("scv"))
```

```python
import functools
import math

import jax
import jax.numpy as jnp
from jax import lax
from jax.experimental import pallas as pl
from jax.experimental.pallas import tpu as pltpu

F32 = jnp.float32
BF16 = jnp.bfloat16
EPS = 1e-6
LANES = 128
SUBLANES = 8
VMEM_LIMIT = 56 * 1024 * 1024
GATE_PAD = LANES
CACHE_PAD = 32


def _pick(n, pref):
    t = min(pref, n)
    while n % t:
        t //= 2
    return t


def _params(*sem):
    return pltpu.CompilerParams(dimension_semantics=sem, vmem_limit_bytes=VMEM_LIMIT)


def _rms(x, g):
    return x * lax.rsqrt(jnp.mean(x * x, axis=-1, keepdims=True) + EPS) * g


def _log_sigmoid(x):
    return jnp.minimum(x, 0.0) - jnp.log1p(jnp.exp(-jnp.abs(x)))


def _dot(a, b):
    return jnp.dot(a, b, preferred_element_type=F32)


def _gate_act(gt, n_heads):
    lane = lax.broadcasted_iota(jnp.int32, gt.shape, 1)
    return jnp.where(lane >= n_heads, _log_sigmoid(gt), gt)


def _rms_gates_kernel(x_ref, g_ref, wif_ref, bias_ref, *rest, n_heads):
    h_ref, gate_ref = rest[-2:]
    hb = _rms(x_ref[...], g_ref[...]).astype(BF16)
    h_ref[...] = hb
    gate_ref[...] = _gate_act(_dot(hb, wif_ref[...]) + bias_ref[...], n_heads)


def rms_gates(x, g, wif, bias, n_heads, prev, row_off, n_rows):
    n, d = x.shape
    tm = _pick(n, 256)
    r0 = row_off // tm
    alias = [] if prev is None else list(prev)
    return pl.pallas_call(
        functools.partial(_rms_gates_kernel, n_heads=n_heads),
        grid=(n // tm,),
        input_output_aliases={4: 0, 5: 1} if alias else {},
        in_specs=[pl.BlockSpec((tm, d), lambda i: (i, 0)),
                  pl.BlockSpec((1, d), lambda i: (0, 0)),
                  pl.BlockSpec((d, GATE_PAD), lambda i: (0, 0)),
                  pl.BlockSpec((1, GATE_PAD), lambda i: (0, 0))]
        + [pl.BlockSpec(memory_space=pl.ANY)] * len(alias),
        out_specs=[pl.BlockSpec((tm, d), lambda i: (r0 + i, 0)),
                   pl.BlockSpec((tm, GATE_PAD), lambda i: (r0 + i, 0))],
        out_shape=[jax.ShapeDtypeStruct((n_rows, d), BF16),
                   jax.ShapeDtypeStruct((n_rows, GATE_PAD), F32)],
        compiler_params=_params("parallel"), name="rms_gates",
    )(x, g, wif, bias, *alias)


def _mm_kernel(a_ref, b_ref, o_ref):
    o_ref[...] = _dot(a_ref[...], b_ref[...]).astype(o_ref.dtype)


def mm(a, b, out_dtype):
    m, k = a.shape
    n = b.shape[1]
    tm, tn = _pick(m, 1024), _pick(n, 1024)
    return pl.pallas_call(
        _mm_kernel,
        grid=(m // tm, n // tn),
        in_specs=[pl.BlockSpec((tm, k), lambda i, j: (i, 0)),
                  pl.BlockSpec((k, tn), lambda i, j: (0, j))],
        out_specs=pl.BlockSpec((tm, tn), lambda i, j: (i, j)),
        out_shape=jax.ShapeDtypeStruct((m, n), out_dtype),
        compiler_params=_params("parallel", "arbitrary"), name="mm_in",
    )(a, b)


def _mm_swiglu_kernel(a_ref, bg_ref, bu_ref, o_ref, *, d_ff):
    a = a_ref[...]
    gate = _dot(a, bg_ref[...])
    up = _dot(a, bu_ref[...])
    f = gate * jax.nn.sigmoid(gate) * up
    j, tn = pl.program_id(1), f.shape[1]
    ragged = (j + 1) * tn > d_ff

    @pl.when(jnp.logical_not(ragged))
    def _():
        o_ref[...] = f.astype(o_ref.dtype)

    @pl.when(ragged)
    def _():
        col = j * tn + lax.broadcasted_iota(jnp.int32, f.shape, 1)
        o_ref[...] = jnp.where(col < d_ff, f, 0.0).astype(o_ref.dtype)


def mm_swiglu(a, bg, bu, layer, n):
    m, k = a.shape
    d_ff = bg.shape[2]
    tm, tn = _pick(m, 1024), _pick(n, 512)
    wspec = pl.BlockSpec((None, k, tn), lambda i, j: (layer, 0, jnp.minimum(j, (d_ff - 1) // tn)))
    return pl.pallas_call(
        functools.partial(_mm_swiglu_kernel, d_ff=d_ff),
        grid=(m // tm, n // tn),
        in_specs=[pl.BlockSpec((tm, k), lambda i, j: (i, 0)), wspec, wspec],
        out_specs=pl.BlockSpec((tm, tn), lambda i, j: (i, j)),
        out_shape=jax.ShapeDtypeStruct((m, n), BF16),
        compiler_params=_params("parallel", "arbitrary"), name="mm_swiglu",
    )(a, bg, bu)


EPI_CHUNKS = 8
MXU_DEPTH = 256


def _mm_res_kernel(*refs, n_a, split, n_tiles, n_steps, tn, rb, ple, second, gates, n_heads, n_x, n_xo, p_chunks):
    refs = list(refs)
    a_refs = [refs.pop(0) for _ in range(n_a)]
    b_ref = refs.pop(0)
    x_refs = [refs.pop(0) for _ in range(n_x)]
    g_ref = refs.pop(0)
    g2_ref = refs.pop(0) if second == "norm" else None
    pe_ref, wple_ref = (refs.pop(0), refs.pop(0)) if ple else (None, None)
    wif_ref, bif_ref = (refs.pop(0), refs.pop(0)) if gates else (None, None)
    xo_refs = [refs.pop(0) for _ in range(n_xo)]
    h_ref = refs.pop(0) if second else None
    gate_ref = refs.pop(0) if gates else None
    acc_refs = (refs.pop(0), refs.pop(0))
    i, s = pl.program_id(0), pl.program_id(1)
    in_first = jnp.where(i == 0, 0, (i - 1) * EPI_CHUNKS + jnp.minimum(s, EPI_CHUNKS - 1)) < p_chunks

    def matmul(acc_ref):
        if split == "n":
            part, k0 = None, 0
            for a_ref in a_refs:
                kw = a_ref.shape[1]
                d = _dot(a_ref[...], b_ref[k0:k0 + kw, :])
                part = d if part is None else part + d
                k0 += kw
            if ple:
                part = _dot(pe_ref[...], wple_ref[...]) * jax.nn.sigmoid(part)
            acc_ref[:, pl.ds(pl.multiple_of(s * tn, tn), tn)] = part
        else:
            acc_ref[...] += _dot(a_refs[0][...], b_ref[...])

    def epilogue(acc_ref):
        rows = pl.ds(pl.multiple_of(s * rb, rb), rb)
        f = acc_ref[rows, :]
        if split == "k":
            acc_ref[rows, :] = jnp.zeros_like(f)
        x = x_refs[0][...] if n_x == 1 else jnp.where(in_first, x_refs[0][...], x_refs[1][...])
        xn = x + _rms(f, g_ref[...])
        if n_xo == 1:
            xo_refs[0][...] = xn
        else:
            xo_refs[0][...] = jnp.where(in_first, xn, xo_refs[0][...])
            xo_refs[1][...] = jnp.where(in_first, xo_refs[1][...], xn)
        if second == "norm":
            hb = _rms(xn, g2_ref[...]).astype(BF16)
            h_ref[...] = hb
            if gates:
                gate_ref[...] = _gate_act(_dot(hb, wif_ref[...]) + bif_ref[...], n_heads)
        elif second == "cast":
            h_ref[...] = xn.astype(BF16)

    @pl.when((i == 0) & (s == 0))
    def _():
        for ref in list(acc_refs) + (xo_refs if n_xo == 2 else []):
            ref[...] = jnp.zeros_like(ref)

    for parity in range(2):
        mine = (i < n_tiles) & (lax.rem(i, 2) == parity)

        @pl.when(mine & (s < EPI_CHUNKS))
        def _(parity=parity):
            epilogue(acc_refs[1 - parity])
            matmul(acc_refs[parity])

        if n_steps > EPI_CHUNKS:
            @pl.when(mine & (s >= EPI_CHUNKS))
            def _(parity=parity):
                matmul(acc_refs[parity])

    @pl.when((i == n_tiles) & (s < EPI_CHUNKS))
    def _():
        epilogue(acc_refs[1 - n_tiles % 2])


def mm_res(a_parts, b, x, g, *, split, second=None, g2=None, pe=None, wple=None, wif=None, bif=None,
           n_heads=0, tm_pref=512, split_out=None, layer=0, name="mm_res"):
    xs = list(x) if isinstance(x, (tuple, list)) else [x]
    m = sum(v.shape[0] for v in xs)
    kdim, n = b.shape[1:]
    tm = _pick(m, tm_pref)
    n_tiles, nch = m // tm, EPI_CHUNKS
    rb = tm // nch
    ple, gates = pe is not None, wif is not None
    once = pl.Buffered(1)
    const = lambda shape: pl.BlockSpec(shape, lambda i, s: (0, 0), pipeline_mode=once)
    tile = lambda i: jnp.minimum(i, n_tiles - 1)
    chunk = lambda i, s: jnp.where(i == 0, 0, (i - 1) * nch + jnp.minimum(s, nch - 1))
    chunk_spec = lambda w: pl.BlockSpec((rb, w), lambda i, s: (chunk(i, s), 0))
    first_rows = xs[0].shape[0] if len(xs) == 2 else (split_out or 0)
    assert first_rows % rb == 0 and (len(xs) == 1 or not split_out or split_out == xs[0].shape[0])
    p_chunks = first_rows // rb
    group_specs = [pl.BlockSpec((rb, n), lambda i, s: (jnp.minimum(chunk(i, s), p_chunks - 1), 0)),
                   pl.BlockSpec((rb, n), lambda i, s: (jnp.maximum(chunk(i, s) - p_chunks, 0), 0))]
    if split == "n":
        n_steps = nch
        tn = n // n_steps
        in_specs = [pl.BlockSpec((tm, a.shape[1]), lambda i, s: (tile(i), 0)) for a in a_parts]
        in_specs.append(pl.BlockSpec((None, kdim, tn), lambda i, s: (layer, 0, s)))
    else:
        (a,) = a_parts
        tn = n
        tk = max(t for t in range(MXU_DEPTH, kdim // nch + 1, MXU_DEPTH) if kdim % t == 0)
        n_steps = kdim // tk
        in_specs = [pl.BlockSpec((tm, tk), lambda i, s: (tile(i), s)),
                    pl.BlockSpec((None, tk, n), lambda i, s: (layer, s, 0))]
    in_specs += (group_specs if len(xs) == 2 else [chunk_spec(n)]) + [const((1, n))]
    args = list(a_parts) + [b] + xs + [g]
    if second == "norm":
        in_specs.append(const((1, n)))
        args.append(g2)
    if ple:
        assert split == "n"
        in_specs += [pl.BlockSpec((tm, pe.shape[1]), lambda i, s: (tile(i), 0)),
                     pl.BlockSpec((None, wple.shape[1], tn), lambda i, s: (layer, 0, s))]
        args += [pe, wple]
    if gates:
        in_specs += [const(wif.shape), const(bif.shape)]
        args += [wif, bif]
    if split_out:
        out_specs = list(group_specs)
        out_shape = [jax.ShapeDtypeStruct((split_out, n), F32), jax.ShapeDtypeStruct((m - split_out, n), F32)]
    else:
        out_specs, out_shape = [chunk_spec(n)], [jax.ShapeDtypeStruct((m, n), F32)]
    if second:
        out_specs.append(chunk_spec(n))
        out_shape.append(jax.ShapeDtypeStruct((m, n), BF16))
    if gates:
        out_specs.append(chunk_spec(GATE_PAD))
        out_shape.append(jax.ShapeDtypeStruct((m, GATE_PAD), F32))
    return pl.pallas_call(
        functools.partial(_mm_res_kernel, n_a=len(a_parts), split=split, n_tiles=n_tiles, n_steps=n_steps, tn=tn,
                          rb=rb, ple=ple,
                          second=second, gates=gates, n_heads=n_heads, n_x=len(xs), n_xo=2 if split_out else 1,
                          p_chunks=p_chunks),
        grid=(n_tiles + 1, n_steps),
        in_specs=in_specs, out_specs=out_specs, out_shape=out_shape,
        scratch_shapes=[pltpu.VMEM((tm, n), F32), pltpu.VMEM((tm, n), F32)],
        compiler_params=_params("arbitrary", "arbitrary"), name=name,
    )(*args)


def _split3(x):
    hi = x.astype(BF16)
    r1 = x - hi.astype(F32)
    mid = r1.astype(BF16)
    lo = (r1 - mid.astype(F32)).astype(BF16)
    return hi, mid, lo


def _mlstm_kernel(zq_ref, zk_ref, zv_ref, zo_ref, gate_ref, c0_ref, n0_ref, m0_ref,
                  gmh_ref, *rest, n_heads, dh, t):
    y_ref, c_ref, n_ref, m_ref = rest[-4:]

    @pl.when(pl.program_id(1) == 0)
    def _():
        c_ref[...] = c0_ref[...]
        n_ref[...] = n0_ref[...]
        m_ref[...] = m0_ref[...]

    row = lax.broadcasted_iota(jnp.int32, (t, t), 0)
    col = lax.broadcasted_iota(jnp.int32, (t, t), 1)
    causal = row >= col
    tri = causal.astype(BF16)
    tri_t = (col >= row).astype(BF16)
    g_col = gate_ref[...]
    g_row = g_col.T
    cum_col = sum(_dot(tri, p) for p in _split3(g_col))
    cum_row = sum(_dot(p, tri_t) for p in _split3(g_row))
    scale = dh ** -0.5

    for h in range(n_heads):
        sl = slice(h * dh, (h + 1) * dh)
        q = zq_ref[:, sl]
        k = zk_ref[:, sl]
        v = zv_ref[:, sl]
        i_col = g_col[:, h:h + 1]
        i_row = g_row[h:h + 1, :]
        b_col = cum_col[:, n_heads + h:n_heads + h + 1]
        b_row = cum_row[n_heads + h:n_heads + h + 1, :]
        m_prev = m_ref[0, :, h:h + 1]
        c_prev = c_ref[0, h]
        n_prev = n_ref[0, h:h + 1, :]

        d = jnp.where(causal, b_col - b_row + i_row, -jnp.inf)
        inter = b_col + m_prev
        mt = jnp.maximum(inter, jnp.max(d, axis=1, keepdims=True))
        s = lax.dot_general(q, k, (((1,), (1,)), ((), ())), preferred_element_type=F32)
        s = s * scale * jnp.exp(d - mt)
        sc_t = jnp.exp(inter - mt)
        num = sc_t * _dot(q, c_prev.astype(BF16)) + _dot(s.astype(BF16), v)
        qn = jnp.sum(q.astype(F32) * n_prev, axis=1, keepdims=True)
        den = sc_t * qn + jnp.sum(s, axis=1, keepdims=True)
        hh = num / jnp.maximum(jnp.abs(den), jnp.exp(-mt))

        b_last = b_col[t - 1:t, :]
        m_new = mt[t - 1:t, :]
        wl = jnp.exp(b_last - b_col + i_col - m_new)
        sc = jnp.exp(b_last + m_prev - m_new)
        kw = k.astype(F32) * wl
        kv = lax.dot_general(kw.astype(BF16), v, (((0,), (0,)), ((), ())), preferred_element_type=F32)
        c_ref[0, h] = sc * c_prev + kv * scale
        n_ref[0, h:h + 1, :] = sc * n_prev + jnp.sum(kw, axis=0, keepdims=True) * scale
        m_ref[0, :, h:h + 1] = m_new

        hn = _rms(hh, gmh_ref[:, sl])
        y_ref[:, sl] = (hn * jax.nn.sigmoid(zo_ref[:, sl].astype(F32))).astype(BF16)


def mlstm(z, gates, c0, n0, m0, gmh, y_prev, c_prev, layer, row_off, bsz, seq, n_heads, dh):
    d_a = n_heads * dh
    depth = c0.shape[0]
    n_rows = z.shape[0]
    t = min(256, seq)
    nc = seq // t
    r0 = row_off // t
    rmap = lambda b, c: r0 + b * nc + c
    zspec = lambda j: pl.BlockSpec((t, d_a), lambda b, c, j=j: (rmap(b, c), j))
    alias, io_alias = [], {}
    for prev, out_idx in ((y_prev, 0), (c_prev, 1)):
        if prev is not None:
            io_alias[9 + len(alias)] = out_idx
            alias.append(prev)
    cspec = pl.BlockSpec((None, 1, n_heads, dh, dh), lambda b, c: (layer, b, 0, 0, 0))
    return pl.pallas_call(
        functools.partial(_mlstm_kernel, n_heads=n_heads, dh=dh, t=t),
        grid=(bsz, nc),
        input_output_aliases=io_alias,
        in_specs=[zspec(0), zspec(1), zspec(2), zspec(3),
                  pl.BlockSpec((t, GATE_PAD), lambda b, c: (rmap(b, c), 0)),
                  cspec,
                  pl.BlockSpec((1, n_heads, dh), lambda b, c: (b, 0, 0)),
                  pl.BlockSpec((1, 1, LANES), lambda b, c: (b, 0, 0)),
                  pl.BlockSpec((1, d_a), lambda b, c: (0, 0))]
        + [pl.BlockSpec(memory_space=pl.ANY)] * len(alias),
        out_specs=[pl.BlockSpec((t, d_a), lambda b, c: (rmap(b, c), 0)),
                   cspec,
                   pl.BlockSpec((1, n_heads, dh), lambda b, c: (b, 0, 0)),
                   pl.BlockSpec((1, 1, LANES), lambda b, c: (b, 0, 0))],
        out_shape=[jax.ShapeDtypeStruct((n_rows, d_a), BF16),
                   jax.ShapeDtypeStruct((depth, bsz, n_heads, dh, dh), F32),
                   jax.ShapeDtypeStruct((bsz, n_heads, dh), F32),
                   jax.ShapeDtypeStruct((bsz, 1, LANES), F32)],
        compiler_params=_params("parallel", "arbitrary"), name="mlstm",
    )(z, z, z, z, gates, c0, n0, m0, gmh, *alias)


S5_ROWS = 512


def _s5_disc_kernel(lr_ref, li_ref, ldt_ref, br_ref, bi_ref, abr_ref, abi_ref, bbr_ref, bbi_ref):
    dt = jnp.exp(ldt_ref[...])
    lr = lr_ref[...]
    li = li_ref[...]
    mag = jnp.exp(lr * dt)
    ab_re = mag * jnp.cos(li * dt)
    ab_im = mag * jnp.sin(li * dt)
    den = lr * lr + li * li
    nr = ab_re - 1.0
    coef_re = (nr * lr + ab_im * li) / den
    coef_im = (ab_im * lr - nr * li) / den
    br = br_ref[...]
    bi = bi_ref[...]
    abr_ref[...] = ab_re
    abi_ref[...] = ab_im
    bbr_ref[...] = coef_re * br - coef_im * bi
    bbi_ref[...] = coef_re * bi + coef_im * br


def s5_discretise(lam_re, lam_im, log_dt, b_re, b_im):
    g, p, j = b_re.shape
    rep = lambda a: jnp.repeat(a, j, axis=0)
    br_t = jnp.swapaxes(b_re, 1, 2).reshape(g * j, p)
    bi_t = jnp.swapaxes(b_im, 1, 2).reshape(g * j, p)
    shp = jax.ShapeDtypeStruct((g * j, p), F32)
    abr, abi, bbr, bbi = pl.pallas_call(
        _s5_disc_kernel, out_shape=[shp, shp, shp, shp], name="s5_disc",
    )(rep(lam_re), rep(lam_im), rep(log_dt[:, None]), br_t, bi_t)
    first = lambda a: a.reshape(g, j, p)[:, 0]
    return first(abr), first(abi), bbr.reshape(g, j, p), bbi.reshape(g, j, p)


def _s5_scan_kernel(*refs, s_rows, tc, nq, gw, slab, emit_y):
    if emit_y:
        u_ref, h0_ref, a_ref, bblk_ref, cblk_ref, d_ref, wglu_ref, y_ref, hout_ref, bu_scr = refs
    else:
        u_ref, h0_ref, a_ref, bblk_ref, hout_ref, bu_scr = refs

    @pl.when(pl.program_id(0) == 0)
    def _():
        hout_ref[...] = h0_ref[...]

    u = u_ref[...]
    kq = u.shape[1] // nq
    for q in range(nq):
        bu_scr[:, q * 2 * gw:(q + 1) * 2 * gw] = _dot(u[:, q * kq:(q + 1) * kq], bblk_ref[q])

    for q in range(nq):
        for lo in range(0, gw, slab):
            re = slice(q * 2 * gw + lo, q * 2 * gw + lo + slab)
            im = slice(q * 2 * gw + gw + lo, q * 2 * gw + gw + lo + slab)
            ar = a_ref[:, re]
            ai = a_ref[:, im]

            def body(step, carry, re=re, im=im, ar=ar, ai=ai):
                hr, hi = carry
                rows = pl.ds(pl.multiple_of(step * s_rows, s_rows), s_rows)
                nr = ar * hr - ai * hi + bu_scr[rows, re]
                ni = ar * hi + ai * hr + bu_scr[rows, im]
                if emit_y:
                    bu_scr[rows, re] = nr
                    bu_scr[rows, im] = ni
                return nr, ni

            hr, hi = lax.fori_loop(0, tc, body, (hout_ref[:, re], hout_ref[:, im]), unroll=2)
            hout_ref[:, re] = hr
            hout_ref[:, im] = hi

    if emit_y:
        ys = [_dot(bu_scr[:, q * 2 * gw:(q + 1) * 2 * gw].astype(BF16), cblk_ref[q]) for q in range(nq)]
        y = jnp.concatenate(ys, axis=1) + d_ref[...] * u.astype(F32)
        zg = 0.5 * y * (1.0 + jnp.tanh(math.sqrt(2.0 / math.pi) * (y + 0.044715 * (y * y * y))))
        y_ref[...] = (zg * jax.nn.sigmoid(_dot(zg.astype(BF16), wglu_ref[...]))).astype(BF16)


def s5_scan(u_perm, h0, a_blk, bblk, cblk, d, wglu, s_rows, emit_y):
    rows, d_b = u_perm.shape
    nq, kq, w2 = bblk.shape
    gw = w2 // 2
    steps = rows // s_rows
    tc = _pick(steps, max(1, S5_ROWS // s_rows))
    rb = tc * s_rows
    wtot = nq * w2
    const2 = lambda c: (0, 0)
    in_specs = [pl.BlockSpec((rb, d_b), lambda c: (c, 0)),
                pl.BlockSpec((s_rows, wtot), const2),
                pl.BlockSpec((1, wtot), const2),
                pl.BlockSpec((nq, kq, w2), lambda c: (0, 0, 0))]
    args = [u_perm, h0, a_blk, bblk]
    out_specs = [pl.BlockSpec((s_rows, wtot), const2)]
    out_shape = [jax.ShapeDtypeStruct((s_rows, wtot), F32)]
    if emit_y:
        in_specs += [pl.BlockSpec((nq, w2, kq), lambda c: (0, 0, 0)),
                     pl.BlockSpec((1, d_b), const2),
                     pl.BlockSpec((d_b, d_b), const2)]
        args += [cblk, d, wglu]
        out_specs = [pl.BlockSpec((rb, d_b), lambda c: (c, 0))] + out_specs
        out_shape = [jax.ShapeDtypeStruct((rows, d_b), BF16)] + out_shape
    return pl.pallas_call(
        functools.partial(_s5_scan_kernel, s_rows=s_rows, tc=tc, nq=nq, gw=gw,
                          slab=min(512, gw), emit_y=emit_y),
        grid=(steps // tc,),
        in_specs=in_specs, out_specs=out_specs, out_shape=out_shape,
        scratch_shapes=[pltpu.VMEM((rb, wtot), F32)],
        compiler_params=_params("arbitrary"), name="s5_scan" if emit_y else "s5_ends",
    )(*args)


def _s5_chain_kernel(sfin_ref, a_ref, hin_ref, *, seg_len, n_seg, nq, gw):
    for q in range(nq):
        re = slice(q * 2 * gw, q * 2 * gw + gw)
        im = slice(q * 2 * gw + gw, (q + 1) * 2 * gw)
        br, bi = a_ref[:, re], a_ref[:, im]
        pr, pi = jnp.ones_like(br), jnp.zeros_like(br)
        e = seg_len
        while e:
            if e & 1:
                pr, pi = pr * br - pi * bi, pr * bi + pi * br
            br, bi = br * br - bi * bi, 2.0 * br * bi
            e >>= 1
        hr, hi = jnp.zeros_like(pr), jnp.zeros_like(pr)
        for s in range(n_seg):
            hin_ref[s:s + 1, re] = hr
            hin_ref[s:s + 1, im] = hi
            hr, hi = (pr * hr - pi * hi + sfin_ref[s:s + 1, re],
                      pr * hi + pi * hr + sfin_ref[s:s + 1, im])


def s5_chain(sfin, a_blk, seg_len, nq):
    n_seg, wtot = sfin.shape
    return pl.pallas_call(
        functools.partial(_s5_chain_kernel, seg_len=seg_len, n_seg=n_seg, nq=nq, gw=wtot // nq // 2),
        out_shape=jax.ShapeDtypeStruct((n_seg, wtot), F32), name="s5_chain",
    )(sfin, a_blk)


def _conv_kernel(val_ref, gate_ref, cache_ref, w_ref, b_ref, lng_ref, lnb_ref, *rest, t, width):
    y_ref, cout_ref, xp_scr, xs_scr = rest[-4:]
    c = pl.program_id(1)

    @pl.when(c == 0)
    def _():
        xp_scr[0:CACHE_PAD, :] = cache_ref[0]

    @pl.when(c > 0)
    def _():
        xp_scr[0:CACHE_PAD, :] = xp_scr[t:t + CACHE_PAD, :]

    xp_scr[CACHE_PAD:CACHE_PAD + t, :] = val_ref[...].astype(F32) * jax.nn.sigmoid(gate_ref[...].astype(F32))
    first = CACHE_PAD - (width - 1)
    acc = None
    for r in range(min(SUBLANES, width)):
        taps = range(r, width, SUBLANES)
        rows = t + SUBLANES * (len(taps) - 1)
        xs_scr[0:rows, :] = xp_scr[first + r:first + r + rows, :]
        for a, j in enumerate(taps):
            term = w_ref[j:j + 1, :] * xs_scr[SUBLANES * a:SUBLANES * a + t, :]
            acc = term if acc is None else acc + term
    y = acc + b_ref[...]
    mu = jnp.mean(y, axis=-1, keepdims=True)
    yc = y - mu
    var = jnp.mean(yc * yc, axis=-1, keepdims=True)
    y = yc * lax.rsqrt(var + EPS) * lng_ref[...] + lnb_ref[...]
    y_ref[...] = (y * jax.nn.sigmoid(y)).astype(BF16)
    cout_ref[0] = xp_scr[t:t + CACHE_PAD, :]


def conv_module(z, cache, w, b, lng, lnb, y_prev, row_off, bsz, seq, col_val, col_gate):
    width, d_c = w.shape
    alias = [] if y_prev is None else [y_prev]
    t = min(256, seq)
    assert t >= CACHE_PAD
    nc = seq // t
    r0 = row_off // t
    rmap = lambda bb, c: r0 + bb * nc + c
    vec = pl.BlockSpec((1, d_c), lambda bb, c: (0, 0))
    return pl.pallas_call(
        functools.partial(_conv_kernel, t=t, width=width),
        grid=(bsz, nc),
        input_output_aliases={7: 0} if alias else {},
        in_specs=[pl.BlockSpec((t, d_c), lambda bb, c: (rmap(bb, c), col_val)),
                  pl.BlockSpec((t, d_c), lambda bb, c: (rmap(bb, c), col_gate)),
                  pl.BlockSpec((1, CACHE_PAD, d_c), lambda bb, c: (bb, 0, 0)),
                  pl.BlockSpec((width, d_c), lambda bb, c: (0, 0)),
                  vec, vec, vec] + [pl.BlockSpec(memory_space=pl.ANY)] * len(alias),
        out_specs=[pl.BlockSpec((t, d_c), lambda bb, c: (rmap(bb, c), 0)),
                   pl.BlockSpec((1, CACHE_PAD, d_c), lambda bb, c: (bb, 0, 0))],
        out_shape=[jax.ShapeDtypeStruct((z.shape[0], d_c), BF16),
                   jax.ShapeDtypeStruct((bsz, CACHE_PAD, d_c), F32)],
        scratch_shapes=[pltpu.VMEM((CACHE_PAD + t, d_c), F32), pltpu.VMEM((CACHE_PAD + t, d_c), F32)],
        compiler_params=_params("parallel", "arbitrary"), name="conv",
    )(z, z, cache, w, b, lng, lnb, *alias)


def _block_diag(w, gq):
    g, r, c = w.shape
    eye = jnp.eye(gq, dtype=w.dtype)
    w = w.reshape(g // gq, gq, r, c)
    return jnp.einsum("qgrc,gh->qgrhc", w, eye).reshape(g // gq, gq * r, gq * c)


def _state_to_lanes(re, im, gq):
    s, g, p = re.shape
    both = jnp.stack([re.reshape(s, g // gq, gq * p), im.reshape(s, g // gq, gq * p)], axis=2)
    return both.reshape(s, -1)


def _lanes_to_state(h, g, p, gq):
    s = h.shape[0]
    both = h.reshape(s, g // gq, 2, gq * p)
    return both[:, :, 0].reshape(s, g, p), both[:, :, 1].reshape(s, g, p)


def _to_step_major(x, n_seq):
    rows, d = x.shape
    return x.reshape(n_seq, rows // n_seq, d).swapaxes(0, 1).reshape(rows, d)


def _from_step_major(x, n_seq):
    rows, d = x.shape
    return x.reshape(rows // n_seq, n_seq, d).swapaxes(0, 1).reshape(rows, d)


PROMPT_SEGMENTS = 32


def kernel(x_prompt, x_sample, p_prompt, p_sample, state_mlstm_c, state_mlstm_n, state_mlstm_m, state_s5_re, state_s5_im, cache_conv, g_pre_mix, w_in, b_igate, b_fgate, g_mh, s5_lam_re, s5_lam_im, s5_log_dt, s5_b_re, s5_b_im, s5_c_re, s5_c_im, s5_d, s5_w_glu, conv_w, conv_b, conv_ln_g, conv_ln_b, w_out, g_post_mix, g_pre_ffn, w_ffn_gate, w_ffn_up, w_ffn_down, g_post_ffn, w_ple, w_ple_gate, g_post_ple):
    depth = w_in.shape[0]
    bp, lp, d_model = x_prompt.shape
    bs, ls, _ = x_sample.shape
    n_heads = b_igate.shape[1]
    d_a = g_mh.shape[1]
    dh = d_a // n_heads
    g_b, p_b, j_b = s5_b_re.shape[1:]
    d_b = g_b * j_b
    width, d_c = conv_w.shape[1:]
    d_ff = w_ffn_gate.shape[2]
    assert bp == 1 and 2 * n_heads <= GATE_PAD and d_b == d_c and d_a % d_b == 0
    gq = 256 // j_b
    nq = g_b // gq
    np_rows, ns_rows = bp * lp, bs * ls
    n_seg = min(PROMPT_SEGMENTS, lp // 8)
    seg_len = lp // n_seg
    col_u = 4 * d_a // d_b
    col_val, col_gate = col_u + 1, col_u + 2
    d_ff_pad = -(-d_ff // 1024) * 1024

    n_rows = np_rows + ns_rows
    x = (x_prompt.reshape(np_rows, d_model), x_sample.reshape(ns_rows, d_model))
    row = lambda v: v.reshape(1, -1).astype(F32)
    zeros = lambda *s: jnp.zeros(s, F32)
    pad_m = lambda m: jnp.pad(m, ((0, 0), (0, LANES - n_heads)))[:, None, :]
    pad_cache = lambda c: jnp.pad(c, ((0, 0), (CACHE_PAD - (width - 1), 0), (0, 0)))

    n_qkvo = 4 * d_a
    w_ifs = [jnp.pad(w_in[i][:, n_qkvo:n_qkvo + 2 * n_heads], ((0, 0), (0, GATE_PAD - 2 * n_heads))).astype(BF16)
             for i in range(depth)]
    b_ifs = [jnp.pad(jnp.concatenate([b_igate[i], b_fgate[i]]), (0, GATE_PAD - 2 * n_heads)).reshape(1, GATE_PAD)
             for i in range(depth)]
    w_gate_all, w_up_all = w_ffn_gate.astype(BF16), w_ffn_up.astype(BF16)
    w_down_all = jnp.pad(w_ffn_down.astype(BF16), ((0, 0), (0, d_ff_pad - d_ff), (0, 0)))
    w_out_all, w_pgate_all, w_ple_all = w_out.astype(BF16), w_ple_gate.astype(BF16), w_ple.astype(BF16)
    zero_c = zeros(depth, bp, n_heads, dh, dh)
    c_p = c_s = None
    outs = {k: [] for k in ("pn", "pm", "pre", "pim", "pcv", "sn", "sm", "sre", "sim", "scv")}
    for i in range(depth):
        w_main = jnp.concatenate([w_in[i][:, :n_qkvo], w_in[i][:, n_qkvo + 2 * n_heads:]], axis=1).astype(BF16)

        ab_re, ab_im, bb_re, bb_im = s5_discretise(s5_lam_re[i], s5_lam_im[i], s5_log_dt[i], s5_b_re[i], s5_b_im[i])
        bblk = jnp.concatenate([_block_diag(bb_re, gq), _block_diag(bb_im, gq)], axis=2).astype(BF16)
        c_re_t = jnp.swapaxes(s5_c_re[i], 1, 2)
        c_im_t = jnp.swapaxes(s5_c_im[i], 1, 2)
        cblk = jnp.concatenate([_block_diag(c_re_t, gq), _block_diag(-c_im_t, gq)], axis=1).astype(BF16)
        a_blk = _state_to_lanes(ab_re[None], ab_im[None], gq)
        wglu = s5_w_glu[i].astype(BF16)
        d_row = row(s5_d[i])

        if i == 0:
            hg = rms_gates(x[0], row(g_pre_mix[i]), w_ifs[i], b_ifs[i], n_heads, None, 0, n_rows)
            h, gates = rms_gates(x[1], row(g_pre_mix[i]), w_ifs[i], b_ifs[i], n_heads, hg, np_rows, n_rows)
        z = mm(h, w_main, BF16)

        y_a, c_p, n1, m1 = mlstm(z, gates, zero_c, zeros(bp, n_heads, dh), zeros(bp, 1, LANES), row(g_mh[i]),
                                 None, c_p, i, 0, bp, lp, n_heads, dh)
        y_a, c_s, n2, m2 = mlstm(z, gates, state_mlstm_c, state_mlstm_n[i], pad_m(state_mlstm_m[i]), row(g_mh[i]),
                                 y_a, c_s, i, np_rows, bs, ls, n_heads, dh)

        u_all = z[:, 4 * d_a:4 * d_a + d_b]
        u_p = _to_step_major(u_all[:np_rows], n_seg)
        u_s = _to_step_major(u_all[np_rows:], bs)
        wtot = a_blk.shape[1]
        (sfin,) = s5_scan(u_p, zeros(n_seg, wtot), a_blk, bblk, None, None, None, n_seg, False)
        hin = s5_chain(sfin, a_blk, seg_len, nq)
        yb_p, hfin_p = s5_scan(u_p, hin, a_blk, bblk, cblk, d_row, wglu, n_seg, True)
        yb_s, hfin_s = s5_scan(u_s, _state_to_lanes(state_s5_re[i], state_s5_im[i], gq), a_blk, bblk, cblk,
                               d_row, wglu, bs, True)
        y_b = jnp.concatenate([_from_step_major(yb_p, n_seg), _from_step_major(yb_s, bs)], axis=0)
        re_p, im_p = _lanes_to_state(hfin_p[n_seg - 1:], g_b, p_b, gq)
        re_s, im_s = _lanes_to_state(hfin_s, g_b, p_b, gq)

        cw, cb, lg, lb = conv_w[i], row(conv_b[i]), row(conv_ln_g[i]), row(conv_ln_b[i])
        y_c, cv_p = conv_module(z, zeros(bp, CACHE_PAD, d_c), cw, cb, lg, lb, None, 0, bp, lp, col_val, col_gate)
        y_c, cv_s = conv_module(z, pad_cache(cache_conv[i]), cw, cb, lg, lb, y_c, np_rows, bs, ls, col_val, col_gate)

        x, h = mm_res([y_a, y_b, y_c], w_out_all, x, row(g_post_mix[i]), split="n", layer=i,
                      second="norm", g2=row(g_pre_ffn[i]), name="mm_out")

        f = mm_swiglu(h, w_gate_all, w_up_all, i, d_ff_pad)
        x, xb = mm_res([f], w_down_all, x, row(g_post_ffn[i]), split="k", layer=i, second="cast",
                       name="mm_down")

        pe = jnp.concatenate([p_prompt[i].reshape(np_rows, -1), p_sample[i].reshape(ns_rows, -1)], axis=0).astype(BF16)
        ple_args = dict(split="n", layer=i, pe=pe, wple=w_ple_all, name="mm_ple")
        if i + 1 < depth:
            x, h, gates = mm_res([xb], w_pgate_all, x, row(g_post_ple[i]), second="norm",
                                 g2=row(g_pre_mix[i + 1]), wif=w_ifs[i + 1], bif=b_ifs[i + 1], n_heads=n_heads,
                                 **ple_args)
        else:
            x = mm_res([xb], w_pgate_all, x, row(g_post_ple[i]), split_out=np_rows, **ple_args)

        first = CACHE_PAD - (width - 1)
        outs["pn"].append(n1); outs["pm"].append(m1[:, 0, :n_heads])
        outs["pre"].append(re_p); outs["pim"].append(im_p); outs["pcv"].append(cv_p[:, first:])
        outs["sn"].append(n2); outs["sm"].append(m2[:, 0, :n_heads])
        outs["sre"].append(re_s); outs["sim"].append(im_s); outs["scv"].append(cv_s[:, first:])

    st = lambda k: jnp.stack(outs[k])
    return (x[0].reshape(bp, lp, d_model), x[1].reshape(bs, ls, d_model),
            c_p, st("pn"), st("pm"), st("pre"), st("pim"), st("pcv"),
            c_s, st("sn"), st("sm"), st("sre"), st("sim"), st("scv"))
```

```python
import functools
import math

import jax
import jax.numpy as jnp
from jax import lax
from jax.experimental import pallas as pl
from jax.experimental.pallas import tpu as pltpu

F32 = jnp.float32
BF16 = jnp.bfloat16
EPS = 1e-6
LANES = 128
SUBLANES = 8
VMEM_LIMIT = 56 * 1024 * 1024
GATE_PAD = LANES
CACHE_PAD = 32


def _pick(n, pref):
    t = min(pref, n)
    while n % t:
        t //= 2
    return t


def _params(*sem):
    return pltpu.CompilerParams(dimension_semantics=sem, vmem_limit_bytes=VMEM_LIMIT)


def _rms(x, g):
    return x * lax.rsqrt(jnp.mean(x * x, axis=-1, keepdims=True) + EPS) * g


def _log_sigmoid(x):
    return jnp.minimum(x, 0.0) - jnp.log1p(jnp.exp(-jnp.abs(x)))


def _dot(a, b):
    return jnp.dot(a, b, preferred_element_type=F32)


def _gate_act(gt, n_heads):
    lane = lax.broadcasted_iota(jnp.int32, gt.shape, 1)
    return jnp.where(lane >= n_heads, _log_sigmoid(gt), gt)


def _rms_gates_kernel(x0_ref, x1_ref, g_ref, wif_ref, bias_ref, h_ref, gate_ref, *, n_heads, tiles0):
    x = jnp.where(pl.program_id(0) < tiles0, x0_ref[...], x1_ref[...])
    hb = _rms(x, g_ref[...]).astype(BF16)
    h_ref[...] = hb
    gate_ref[...] = _gate_act(_dot(hb, wif_ref[...]) + bias_ref[...], n_heads)


def rms_gates(x0, x1, g, wif, bias, n_heads):
    (n0, d), n1 = x0.shape, x1.shape[0]
    tm = _pick(math.gcd(n0, n1), 256)
    tiles0 = n0 // tm
    return pl.pallas_call(
        functools.partial(_rms_gates_kernel, n_heads=n_heads, tiles0=tiles0),
        grid=((n0 + n1) // tm,),
        in_specs=[pl.BlockSpec((tm, d), lambda i: (jnp.minimum(i, tiles0 - 1), 0)),
                  pl.BlockSpec((tm, d), lambda i: (jnp.maximum(i - tiles0, 0), 0)),
                  pl.BlockSpec((1, d), lambda i: (0, 0)),
                  pl.BlockSpec((d, GATE_PAD), lambda i: (0, 0)),
                  pl.BlockSpec((1, GATE_PAD), lambda i: (0, 0))],
        out_specs=[pl.BlockSpec((tm, d), lambda i: (i, 0)),
                   pl.BlockSpec((tm, GATE_PAD), lambda i: (i, 0))],
        out_shape=[jax.ShapeDtypeStruct((n0 + n1, d), BF16),
                   jax.ShapeDtypeStruct((n0 + n1, GATE_PAD), F32)],
        compiler_params=_params("parallel"), name="rms_gates",
    )(x0, x1, g, wif, bias)


def _mm_kernel(a_ref, b_ref, o_ref):
    o_ref[...] = _dot(a_ref[...], b_ref[...]).astype(o_ref.dtype)


def mm(a, b, out_dtype):
    m, k = a.shape
    n = b.shape[1]
    tm, tn = _pick(m, 1024), _pick(n, 1024)
    return pl.pallas_call(
        _mm_kernel,
        grid=(m // tm, n // tn),
        in_specs=[pl.BlockSpec((tm, k), lambda i, j: (i, 0)),
                  pl.BlockSpec((k, tn), lambda i, j: (0, j))],
        out_specs=pl.BlockSpec((tm, tn), lambda i, j: (i, j)),
        out_shape=jax.ShapeDtypeStruct((m, n), out_dtype),
        compiler_params=_params("parallel", "arbitrary"), name="mm_in",
    )(a, b)


def _mm_swiglu_kernel(a_ref, bg_ref, bu_ref, o_ref, *, d_ff):
    a = a_ref[...]
    gate = _dot(a, bg_ref[...])
    up = _dot(a, bu_ref[...])
    f = gate * jax.nn.sigmoid(gate) * up
    j, tn = pl.program_id(1), f.shape[1]
    ragged = (j + 1) * tn > d_ff

    @pl.when(jnp.logical_not(ragged))
    def _():
        o_ref[...] = f.astype(o_ref.dtype)

    @pl.when(ragged)
    def _():
        col = j * tn + lax.broadcasted_iota(jnp.int32, f.shape, 1)
        o_ref[...] = jnp.where(col < d_ff, f, 0.0).astype(o_ref.dtype)


def mm_swiglu(a, bg, bu, layer, n):
    m, k = a.shape
    d_ff = bg.shape[2]
    tm, tn = _pick(m, 1024), _pick(n, 512)
    wspec = pl.BlockSpec((None, k, tn), lambda i, j: (layer, 0, jnp.minimum(j, (d_ff - 1) // tn)))
    return pl.pallas_call(
        functools.partial(_mm_swiglu_kernel, d_ff=d_ff),
        grid=(m // tm, n // tn),
        in_specs=[pl.BlockSpec((tm, k), lambda i, j: (i, 0)), wspec, wspec],
        out_specs=pl.BlockSpec((tm, tn), lambda i, j: (i, j)),
        out_shape=jax.ShapeDtypeStruct((m, n), BF16),
        compiler_params=_params("parallel", "arbitrary"), name="mm_swiglu",
    )(a, bg, bu)


EPI_CHUNKS = 8
MXU_DEPTH = 256


def _mm_res_kernel(*refs, a_pairs, split, n_tiles, n_steps, nch, tn, rb, ple, second, gates, n_heads, n_x, n_xo,
                   p_chunks, p_tiles):
    refs = list(refs)
    a_refs = [(refs.pop(0), refs.pop(0)) if pair else refs.pop(0) for pair in a_pairs]
    b_ref = refs.pop(0)
    x_refs = [refs.pop(0) for _ in range(n_x)]
    g_ref = refs.pop(0)
    g2_ref = refs.pop(0) if second == "norm" else None
    pe_ref, wple_ref = (refs.pop(0), refs.pop(0)) if ple else (None, None)
    wif_ref, bif_ref = (refs.pop(0), refs.pop(0)) if gates else (None, None)
    xo_refs = [refs.pop(0) for _ in range(n_xo)]
    h_ref = refs.pop(0) if second else None
    gate_ref = refs.pop(0) if gates else None
    acc_refs = (refs.pop(0), refs.pop(0))
    i, s = pl.program_id(0), pl.program_id(1)
    in_first = jnp.where(i == 0, 0, (i - 1) * nch + jnp.minimum(s, nch - 1)) < p_chunks

    def load_a(a_ref):
        if isinstance(a_ref, tuple):
            return jnp.where(i < p_tiles, a_ref[0][...], a_ref[1][...])
        return a_ref[...]

    def matmul(acc_ref):
        if split == "n":
            part, k0 = None, 0
            for a_ref in a_refs:
                a = load_a(a_ref)
                kw = a.shape[1]
                d = _dot(a, b_ref[k0:k0 + kw, :])
                part = d if part is None else part + d
                k0 += kw
            if ple:
                part = _dot(pe_ref[...], wple_ref[...]) * jax.nn.sigmoid(part)
            acc_ref[:, pl.ds(pl.multiple_of(s * tn, tn), tn)] = part
        else:
            acc_ref[...] += _dot(load_a(a_refs[0]), b_ref[...])

    def epilogue(acc_ref):
        rows = pl.ds(pl.multiple_of(s * rb, rb), rb)
        f = acc_ref[rows, :]
        if split == "k":
            acc_ref[rows, :] = jnp.zeros_like(f)
        x = x_refs[0][...] if n_x == 1 else jnp.where(in_first, x_refs[0][...], x_refs[1][...])
        xn = x + _rms(f, g_ref[...])
        if n_xo == 1:
            xo_refs[0][...] = xn
        else:
            xo_refs[0][...] = jnp.where(in_first, xn, xo_refs[0][...])
            xo_refs[1][...] = jnp.where(in_first, xo_refs[1][...], xn)
        if second == "norm":
            hb = _rms(xn, g2_ref[...]).astype(BF16)
            h_ref[...] = hb
            if gates:
                gate_ref[...] = _gate_act(_dot(hb, wif_ref[...]) + bif_ref[...], n_heads)
        elif second == "cast":
            h_ref[...] = xn.astype(BF16)

    @pl.when((i == 0) & (s == 0))
    def _():
        for ref in list(acc_refs) + (xo_refs if n_xo == 2 else []):
            ref[...] = jnp.zeros_like(ref)

    for parity in range(2):
        mine = (i < n_tiles) & (lax.rem(i, 2) == parity)

        @pl.when(mine & (s < nch))
        def _(parity=parity):
            epilogue(acc_refs[1 - parity])
            matmul(acc_refs[parity])

        if n_steps > nch:
            @pl.when(mine & (s >= nch))
            def _(parity=parity):
                matmul(acc_refs[parity])

    @pl.when((i == n_tiles) & (s < nch))
    def _():
        epilogue(acc_refs[1 - n_tiles % 2])


def mm_res(a_parts, b, x, g, *, split, second=None, g2=None, pe=None, wple=None, wif=None, bif=None,
           n_heads=0, tm_pref=512, split_out=None, layer=0, chunks=EPI_CHUNKS, name="mm_res"):
    xs = list(x) if isinstance(x, (tuple, list)) else [x]
    m = sum(v.shape[0] for v in xs)
    kdim, n = b.shape[1:]
    tm = _pick(m, tm_pref)
    n_tiles, nch = m // tm, chunks
    rb = tm // nch
    ple, gates = pe is not None, wif is not None
    once = pl.Buffered(1)
    const = lambda shape: pl.BlockSpec(shape, lambda i, s: (0, 0), pipeline_mode=once)
    tile = lambda i: jnp.minimum(i, n_tiles - 1)
    chunk = lambda i, s: jnp.where(i == 0, 0, (i - 1) * nch + jnp.minimum(s, nch - 1))
    chunk_spec = lambda w: pl.BlockSpec((rb, w), lambda i, s: (chunk(i, s), 0))
    a_pairs = [isinstance(a, (tuple, list)) for a in a_parts]
    first_rows = {v[0].shape[0] for v in [xs] + [a for a, pair in zip(a_parts, a_pairs) if pair] if len(v) == 2}
    first_rows |= {split_out} if split_out else set()
    assert len(first_rows) <= 1
    first_rows = first_rows.pop() if first_rows else 0
    assert first_rows % tm == 0
    p_chunks, p_tiles = first_rows // rb, first_rows // tm
    group_specs = [pl.BlockSpec((rb, n), lambda i, s: (jnp.minimum(chunk(i, s), p_chunks - 1), 0)),
                   pl.BlockSpec((rb, n), lambda i, s: (jnp.maximum(chunk(i, s) - p_chunks, 0), 0))]
    if split == "n":
        n_steps = nch
        tn = n // n_steps
        in_specs = []
        for a, pair in zip(a_parts, a_pairs):
            if pair:
                kw = a[0].shape[1]
                in_specs += [pl.BlockSpec((tm, kw), lambda i, s: (jnp.minimum(tile(i), p_tiles - 1), 0)),
                             pl.BlockSpec((tm, kw), lambda i, s: (jnp.maximum(tile(i) - p_tiles, 0), 0))]
            else:
                in_specs.append(pl.BlockSpec((tm, a.shape[1]), lambda i, s: (tile(i), 0)))
        in_specs.append(pl.BlockSpec((None, kdim, tn), lambda i, s: (layer, 0, s)))
    else:
        (a,) = a_parts
        assert not a_pairs[0]
        tn = n
        tk = max(t for t in range(MXU_DEPTH, kdim // nch + 1, MXU_DEPTH) if kdim % t == 0)
        n_steps = kdim // tk
        in_specs = [pl.BlockSpec((tm, tk), lambda i, s: (tile(i), s)),
                    pl.BlockSpec((None, tk, n), lambda i, s: (layer, s, 0))]
    in_specs += (group_specs if len(xs) == 2 else [chunk_spec(n)]) + [const((1, n))]
    args = [v for a, pair in zip(a_parts, a_pairs) for v in (a if pair else [a])] + [b] + xs + [g]
    if second == "norm":
        in_specs.append(const((1, n)))
        args.append(g2)
    if ple:
        assert split == "n"
        in_specs += [pl.BlockSpec((tm, pe.shape[1]), lambda i, s: (tile(i), 0)),
                     pl.BlockSpec((None, wple.shape[1], tn), lambda i, s: (layer, 0, s))]
        args += [pe, wple]
    if gates:
        in_specs += [const(wif.shape), const(bif.shape)]
        args += [wif, bif]
    if split_out:
        out_specs = list(group_specs)
        out_shape = [jax.ShapeDtypeStruct((split_out, n), F32), jax.ShapeDtypeStruct((m - split_out, n), F32)]
    else:
        out_specs, out_shape = [chunk_spec(n)], [jax.ShapeDtypeStruct((m, n), F32)]
    if second:
        out_specs.append(chunk_spec(n))
        out_shape.append(jax.ShapeDtypeStruct((m, n), BF16))
    if gates:
        out_specs.append(chunk_spec(GATE_PAD))
        out_shape.append(jax.ShapeDtypeStruct((m, GATE_PAD), F32))
    return pl.pallas_call(
        functools.partial(_mm_res_kernel, a_pairs=tuple(a_pairs), split=split, n_tiles=n_tiles, n_steps=n_steps,
                          nch=nch, tn=tn, rb=rb, ple=ple, second=second, gates=gates, n_heads=n_heads,
                          n_x=len(xs), n_xo=2 if split_out else 1, p_chunks=p_chunks, p_tiles=p_tiles),
        grid=(n_tiles + 1, n_steps),
        in_specs=in_specs, out_specs=out_specs, out_shape=out_shape,
        scratch_shapes=[pltpu.VMEM((tm, n), F32), pltpu.VMEM((tm, n), F32)],
        compiler_params=_params("arbitrary", "arbitrary"), name=name,
    )(*args)


def _split3(x):
    hi = x.astype(BF16)
    r1 = x - hi.astype(F32)
    mid = r1.astype(BF16)
    lo = (r1 - mid.astype(F32)).astype(BF16)
    return hi, mid, lo


def _mlstm_kernel(zq_ref, zk_ref, zv_ref, zo_ref, gate_ref, c0_ref, n0_ref, m0_ref,
                  gmh_ref, y_ref, c_ref, n_ref, m_ref, *, n_heads, dh, t):
    @pl.when(pl.program_id(1) == 0)
    def _():
        c_ref[...] = c0_ref[...]
        n_ref[...] = n0_ref[...]
        m_ref[...] = m0_ref[...]

    row = lax.broadcasted_iota(jnp.int32, (t, t), 0)
    col = lax.broadcasted_iota(jnp.int32, (t, t), 1)
    causal = row >= col
    tri = causal.astype(BF16)
    tri_t = (col >= row).astype(BF16)
    g_col = gate_ref[...]
    g_row = g_col.T
    cum_col = sum(_dot(tri, p) for p in _split3(g_col))
    cum_row = sum(_dot(p, tri_t) for p in _split3(g_row))
    scale = dh ** -0.5

    for h in range(n_heads):
        sl = slice(h * dh, (h + 1) * dh)
        q = zq_ref[:, sl]
        k = zk_ref[:, sl]
        v = zv_ref[:, sl]
        i_col = g_col[:, h:h + 1]
        i_row = g_row[h:h + 1, :]
        b_col = cum_col[:, n_heads + h:n_heads + h + 1]
        b_row = cum_row[n_heads + h:n_heads + h + 1, :]
        m_prev = m_ref[0, :, h:h + 1]
        c_prev = c_ref[0, h]
        n_prev = n_ref[0, h:h + 1, :]

        d = jnp.where(causal, b_col - b_row + i_row, -jnp.inf)
        inter = b_col + m_prev
        mt = jnp.maximum(inter, jnp.max(d, axis=1, keepdims=True))
        s = lax.dot_general(q, k, (((1,), (1,)), ((), ())), preferred_element_type=F32)
        s = s * scale * jnp.exp(d - mt)
        sc_t = jnp.exp(inter - mt)
        num = sc_t * _dot(q, c_prev.astype(BF16)) + _dot(s.astype(BF16), v)
        qn = jnp.sum(q.astype(F32) * n_prev, axis=1, keepdims=True)
        den = sc_t * qn + jnp.sum(s, axis=1, keepdims=True)
        hh = num / jnp.maximum(jnp.abs(den), jnp.exp(-mt))

        b_last = b_col[t - 1:t, :]
        m_new = mt[t - 1:t, :]
        wl = jnp.exp(b_last - b_col + i_col - m_new)
        sc = jnp.exp(b_last + m_prev - m_new)
        kw = k.astype(F32) * wl
        kv = lax.dot_general(kw.astype(BF16), v, (((0,), (0,)), ((), ())), preferred_element_type=F32)
        c_ref[0, h] = sc * c_prev + kv * scale
        n_ref[0, h:h + 1, :] = sc * n_prev + jnp.sum(kw, axis=0, keepdims=True) * scale
        m_ref[0, :, h:h + 1] = m_new

        hn = _rms(hh, gmh_ref[:, sl])
        y_ref[:, sl] = (hn * jax.nn.sigmoid(zo_ref[:, sl].astype(F32))).astype(BF16)


def mlstm(z, gates, c0, n0, m0, gmh, layer, row_off, bsz, seq, n_heads, dh):
    d_a = n_heads * dh
    t = min(256, seq)
    nc = seq // t
    r0 = row_off // t
    rmap = lambda b, c: r0 + b * nc + c
    zspec = lambda j: pl.BlockSpec((t, d_a), lambda b, c, j=j: (rmap(b, c), j))
    return pl.pallas_call(
        functools.partial(_mlstm_kernel, n_heads=n_heads, dh=dh, t=t),
        grid=(bsz, nc),
        in_specs=[zspec(0), zspec(1), zspec(2), zspec(3),
                  pl.BlockSpec((t, GATE_PAD), lambda b, c: (rmap(b, c), 0)),
                  pl.BlockSpec((None, 1, n_heads, dh, dh), lambda b, c: (layer, b, 0, 0, 0)),
                  pl.BlockSpec((1, n_heads, dh), lambda b, c: (b, 0, 0)),
                  pl.BlockSpec((1, 1, LANES), lambda b, c: (b, 0, 0)),
                  pl.BlockSpec((1, d_a), lambda b, c: (0, 0))],
        out_specs=[pl.BlockSpec((t, d_a), lambda b, c: (b * nc + c, 0)),
                   pl.BlockSpec((1, n_heads, dh, dh), lambda b, c: (b, 0, 0, 0)),
                   pl.BlockSpec((1, n_heads, dh), lambda b, c: (b, 0, 0)),
                   pl.BlockSpec((1, 1, LANES), lambda b, c: (b, 0, 0))],
        out_shape=[jax.ShapeDtypeStruct((bsz * seq, d_a), BF16),
                   jax.ShapeDtypeStruct((bsz, n_heads, dh, dh), F32),
                   jax.ShapeDtypeStruct((bsz, n_heads, dh), F32),
                   jax.ShapeDtypeStruct((bsz, 1, LANES), F32)],
        compiler_params=_params("parallel", "arbitrary"), name="mlstm",
    )(z, z, z, z, gates, c0, n0, m0, gmh)


S5_ROWS = 512


def _s5_disc_kernel(lr_ref, li_ref, ldt_ref, br_ref, bi_ref, abr_ref, abi_ref, bbr_ref, bbi_ref):
    dt = jnp.exp(ldt_ref[...])
    lr = lr_ref[...]
    li = li_ref[...]
    mag = jnp.exp(lr * dt)
    ab_re = mag * jnp.cos(li * dt)
    ab_im = mag * jnp.sin(li * dt)
    den = lr * lr + li * li
    nr = ab_re - 1.0
    coef_re = (nr * lr + ab_im * li) / den
    coef_im = (ab_im * lr - nr * li) / den
    br = br_ref[...]
    bi = bi_ref[...]
    abr_ref[...] = ab_re
    abi_ref[...] = ab_im
    bbr_ref[...] = coef_re * br - coef_im * bi
    bbi_ref[...] = coef_re * bi + coef_im * br


def s5_discretise(lam_re, lam_im, log_dt, b_re, b_im):
    g, p, j = b_re.shape
    rep = lambda a: jnp.repeat(a, j, axis=0)
    br_t = jnp.swapaxes(b_re, 1, 2).reshape(g * j, p)
    bi_t = jnp.swapaxes(b_im, 1, 2).reshape(g * j, p)
    shp = jax.ShapeDtypeStruct((g * j, p), F32)
    abr, abi, bbr, bbi = pl.pallas_call(
        _s5_disc_kernel, out_shape=[shp, shp, shp, shp], name="s5_disc",
    )(rep(lam_re), rep(lam_im), rep(log_dt[:, None]), br_t, bi_t)
    first = lambda a: a.reshape(g, j, p)[:, 0]
    return first(abr), first(abi), bbr.reshape(g, j, p), bbi.reshape(g, j, p)


def _s5_scan_kernel(*refs, s_rows, tc, nq, gw, slab, emit_y):
    if emit_y:
        u_ref, h0_ref, a_ref, bblk_ref, cblk_ref, d_ref, wglu_ref, y_ref, hout_ref, bu_scr = refs
    else:
        u_ref, h0_ref, a_ref, bblk_ref, hout_ref, bu_scr = refs

    @pl.when(pl.program_id(0) == 0)
    def _():
        hout_ref[...] = h0_ref[...]

    u = u_ref[...]
    kq = u.shape[1] // nq
    for q in range(nq):
        bu_scr[:, q * 2 * gw:(q + 1) * 2 * gw] = _dot(u[:, q * kq:(q + 1) * kq], bblk_ref[q])

    for q in range(nq):
        for lo in range(0, gw, slab):
            re = slice(q * 2 * gw + lo, q * 2 * gw + lo + slab)
            im = slice(q * 2 * gw + gw + lo, q * 2 * gw + gw + lo + slab)
            ar = a_ref[:, re]
            ai = a_ref[:, im]

            def body(step, carry, re=re, im=im, ar=ar, ai=ai):
                hr, hi = carry
                rows = pl.ds(pl.multiple_of(step * s_rows, s_rows), s_rows)
                nr = ar * hr - ai * hi + bu_scr[rows, re]
                ni = ar * hi + ai * hr + bu_scr[rows, im]
                if emit_y:
                    bu_scr[rows, re] = nr
                    bu_scr[rows, im] = ni
                return nr, ni

            hr, hi = lax.fori_loop(0, tc, body, (hout_ref[:, re], hout_ref[:, im]), unroll=2)
            hout_ref[:, re] = hr
            hout_ref[:, im] = hi

    if emit_y:
        ys = [_dot(bu_scr[:, q * 2 * gw:(q + 1) * 2 * gw].astype(BF16), cblk_ref[q]) for q in range(nq)]
        y = jnp.concatenate(ys, axis=1) + d_ref[...] * u.astype(F32)
        zg = 0.5 * y * (1.0 + jnp.tanh(math.sqrt(2.0 / math.pi) * (y + 0.044715 * (y * y * y))))
        y_ref[...] = (zg * jax.nn.sigmoid(_dot(zg.astype(BF16), wglu_ref[...]))).astype(BF16)


def s5_scan(u_perm, h0, a_blk, bblk, cblk, d, wglu, s_rows, emit_y):
    rows, d_b = u_perm.shape
    nq, kq, w2 = bblk.shape
    gw = w2 // 2
    steps = rows // s_rows
    tc = _pick(steps, max(1, S5_ROWS // s_rows))
    rb = tc * s_rows
    wtot = nq * w2
    const2 = lambda c: (0, 0)
    in_specs = [pl.BlockSpec((rb, d_b), lambda c: (c, 0)),
                pl.BlockSpec((s_rows, wtot), const2),
                pl.BlockSpec((1, wtot), const2),
                pl.BlockSpec((nq, kq, w2), lambda c: (0, 0, 0))]
    args = [u_perm, h0, a_blk, bblk]
    out_specs = [pl.BlockSpec((s_rows, wtot), const2)]
    out_shape = [jax.ShapeDtypeStruct((s_rows, wtot), F32)]
    if emit_y:
        in_specs += [pl.BlockSpec((nq, w2, kq), lambda c: (0, 0, 0)),
                     pl.BlockSpec((1, d_b), const2),
                     pl.BlockSpec((d_b, d_b), const2)]
        args += [cblk, d, wglu]
        out_specs = [pl.BlockSpec((rb, d_b), lambda c: (c, 0))] + out_specs
        out_shape = [jax.ShapeDtypeStruct((rows, d_b), BF16)] + out_shape
    return pl.pallas_call(
        functools.partial(_s5_scan_kernel, s_rows=s_rows, tc=tc, nq=nq, gw=gw,
                          slab=min(512, gw), emit_y=emit_y),
        grid=(steps // tc,),
        in_specs=in_specs, out_specs=out_specs, out_shape=out_shape,
        scratch_shapes=[pltpu.VMEM((rb, wtot), F32)],
        compiler_params=_params("arbitrary"), name="s5_scan" if emit_y else "s5_ends",
    )(*args)


def _s5_chain_kernel(sfin_ref, a_ref, hin_ref, *, seg_len, n_seg, nq, gw):
    for q in range(nq):
        re = slice(q * 2 * gw, q * 2 * gw + gw)
        im = slice(q * 2 * gw + gw, (q + 1) * 2 * gw)
        br, bi = a_ref[:, re], a_ref[:, im]
        pr, pi = jnp.ones_like(br), jnp.zeros_like(br)
        e = seg_len
        while e:
            if e & 1:
                pr, pi = pr * br - pi * bi, pr * bi + pi * br
            br, bi = br * br - bi * bi, 2.0 * br * bi
            e >>= 1
        hr, hi = jnp.zeros_like(pr), jnp.zeros_like(pr)
        for s in range(n_seg):
            hin_ref[s:s + 1, re] = hr
            hin_ref[s:s + 1, im] = hi
            hr, hi = (pr * hr - pi * hi + sfin_ref[s:s + 1, re],
                      pr * hi + pi * hr + sfin_ref[s:s + 1, im])


def s5_chain(sfin, a_blk, seg_len, nq):
    n_seg, wtot = sfin.shape
    return pl.pallas_call(
        functools.partial(_s5_chain_kernel, seg_len=seg_len, n_seg=n_seg, nq=nq, gw=wtot // nq // 2),
        out_shape=jax.ShapeDtypeStruct((n_seg, wtot), F32), name="s5_chain",
    )(sfin, a_blk)


def _conv_kernel(val_ref, gate_ref, cache_ref, w_ref, b_ref, lng_ref, lnb_ref, y_ref, cout_ref, xp_scr, xs_scr,
                 *, t, width):
    c = pl.program_id(1)

    @pl.when(c == 0)
    def _():
        xp_scr[0:CACHE_PAD, :] = cache_ref[0]

    @pl.when(c > 0)
    def _():
        xp_scr[0:CACHE_PAD, :] = xp_scr[t:t + CACHE_PAD, :]

    xp_scr[CACHE_PAD:CACHE_PAD + t, :] = val_ref[...].astype(F32) * jax.nn.sigmoid(gate_ref[...].astype(F32))
    first = CACHE_PAD - (width - 1)
    acc = None
    for r in range(min(SUBLANES, width)):
        taps = range(r, width, SUBLANES)
        rows = t + SUBLANES * (len(taps) - 1)
        xs_scr[0:rows, :] = xp_scr[first + r:first + r + rows, :]
        for a, j in enumerate(taps):
            term = w_ref[j:j + 1, :] * xs_scr[SUBLANES * a:SUBLANES * a + t, :]
            acc = term if acc is None else acc + term
    y = acc + b_ref[...]
    mu = jnp.mean(y, axis=-1, keepdims=True)
    yc = y - mu
    var = jnp.mean(yc * yc, axis=-1, keepdims=True)
    y = yc * lax.rsqrt(var + EPS) * lng_ref[...] + lnb_ref[...]
    y_ref[...] = (y * jax.nn.sigmoid(y)).astype(BF16)
    cout_ref[0] = xp_scr[t:t + CACHE_PAD, :]


def conv_module(z, cache, w, b, lng, lnb, row_off, bsz, seq, col_val, col_gate):
    width, d_c = w.shape
    t = min(256, seq)
    assert t >= CACHE_PAD
    nc = seq // t
    r0 = row_off // t
    rmap = lambda bb, c: r0 + bb * nc + c
    vec = pl.BlockSpec((1, d_c), lambda bb, c: (0, 0))
    return pl.pallas_call(
        functools.partial(_conv_kernel, t=t, width=width),
        grid=(bsz, nc),
        in_specs=[pl.BlockSpec((t, d_c), lambda bb, c: (rmap(bb, c), col_val)),
                  pl.BlockSpec((t, d_c), lambda bb, c: (rmap(bb, c), col_gate)),
                  pl.BlockSpec((1, CACHE_PAD, d_c), lambda bb, c: (bb, 0, 0)),
                  pl.BlockSpec((width, d_c), lambda bb, c: (0, 0)),
                  vec, vec, vec],
        out_specs=[pl.BlockSpec((t, d_c), lambda bb, c: (bb * nc + c, 0)),
                   pl.BlockSpec((1, CACHE_PAD, d_c), lambda bb, c: (bb, 0, 0))],
        out_shape=[jax.ShapeDtypeStruct((bsz * seq, d_c), BF16),
                   jax.ShapeDtypeStruct((bsz, CACHE_PAD, d_c), F32)],
        scratch_shapes=[pltpu.VMEM((CACHE_PAD + t, d_c), F32), pltpu.VMEM((CACHE_PAD + t, d_c), F32)],
        compiler_params=_params("parallel", "arbitrary"), name="conv",
    )(z, z, cache, w, b, lng, lnb)


def _block_diag(w, gq):
    g, r, c = w.shape
    eye = jnp.eye(gq, dtype=w.dtype)
    w = w.reshape(g // gq, gq, r, c)
    return jnp.einsum("qgrc,gh->qgrhc", w, eye).reshape(g // gq, gq * r, gq * c)


def _state_to_lanes(re, im, gq):
    s, g, p = re.shape
    both = jnp.stack([re.reshape(s, g // gq, gq * p), im.reshape(s, g // gq, gq * p)], axis=2)
    return both.reshape(s, -1)


def _lanes_to_state(h, g, p, gq):
    s = h.shape[0]
    both = h.reshape(s, g // gq, 2, gq * p)
    return both[:, :, 0].reshape(s, g, p), both[:, :, 1].reshape(s, g, p)


def _to_step_major(x, n_seq):
    rows, d = x.shape
    return x.reshape(n_seq, rows // n_seq, d).swapaxes(0, 1).reshape(rows, d)


def _from_step_major(x, n_seq):
    rows, d = x.shape
    return x.reshape(rows // n_seq, n_seq, d).swapaxes(0, 1).reshape(rows, d)


PROMPT_SEGMENTS = 32


def kernel(x_prompt, x_sample, p_prompt, p_sample, state_mlstm_c, state_mlstm_n, state_mlstm_m, state_s5_re, state_s5_im, cache_conv, g_pre_mix, w_in, b_igate, b_fgate, g_mh, s5_lam_re, s5_lam_im, s5_log_dt, s5_b_re, s5_b_im, s5_c_re, s5_c_im, s5_d, s5_w_glu, conv_w, conv_b, conv_ln_g, conv_ln_b, w_out, g_post_mix, g_pre_ffn, w_ffn_gate, w_ffn_up, w_ffn_down, g_post_ffn, w_ple, w_ple_gate, g_post_ple):
    depth = w_in.shape[0]
    bp, lp, d_model = x_prompt.shape
    bs, ls, _ = x_sample.shape
    n_heads = b_igate.shape[1]
    d_a = g_mh.shape[1]
    dh = d_a // n_heads
    g_b, p_b, j_b = s5_b_re.shape[1:]
    d_b = g_b * j_b
    width, d_c = conv_w.shape[1:]
    d_ff = w_ffn_gate.shape[2]
    assert bp == 1 and 2 * n_heads <= GATE_PAD and d_b == d_c and d_a % d_b == 0
    gq = 256 // j_b
    nq = g_b // gq
    np_rows, ns_rows = bp * lp, bs * ls
    n_seg = min(PROMPT_SEGMENTS, lp // 8)
    seg_len = lp // n_seg
    col_u = 4 * d_a // d_b
    col_val, col_gate = col_u + 1, col_u + 2
    d_ff_pad = -(-d_ff // 1024) * 1024

    x = (x_prompt.reshape(np_rows, d_model), x_sample.reshape(ns_rows, d_model))
    row = lambda v: v.reshape(1, -1).astype(F32)
    zeros = lambda *s: jnp.zeros(s, F32)
    pad_m = lambda m: jnp.pad(m, ((0, 0), (0, LANES - n_heads)))[:, None, :]
    pad_cache = lambda c: jnp.pad(c, ((0, 0), (CACHE_PAD - (width - 1), 0), (0, 0)))

    n_qkvo = 4 * d_a
    w_ifs = [jnp.pad(w_in[i][:, n_qkvo:n_qkvo + 2 * n_heads], ((0, 0), (0, GATE_PAD - 2 * n_heads))).astype(BF16)
             for i in range(depth)]
    b_ifs = [jnp.pad(jnp.concatenate([b_igate[i], b_fgate[i]]), (0, GATE_PAD - 2 * n_heads)).reshape(1, GATE_PAD)
             for i in range(depth)]
    w_gate_all, w_up_all = w_ffn_gate.astype(BF16), w_ffn_up.astype(BF16)
    w_down_all = jnp.pad(w_ffn_down.astype(BF16), ((0, 0), (0, d_ff_pad - d_ff), (0, 0)))
    w_out_all, w_pgate_all, w_ple_all = w_out.astype(BF16), w_ple_gate.astype(BF16), w_ple.astype(BF16)
    zero_c = zeros(1, bp, n_heads, dh, dh)
    outs = {k: [] for k in ("pc", "sc", "pn", "pm", "pre", "pim", "pcv", "sn", "sm", "sre", "sim", "scv")}
    for i in range(depth):
        w_main = jnp.concatenate([w_in[i][:, :n_qkvo], w_in[i][:, n_qkvo + 2 * n_heads:]], axis=1).astype(BF16)

        ab_re, ab_im, bb_re, bb_im = s5_discretise(s5_lam_re[i], s5_lam_im[i], s5_log_dt[i], s5_b_re[i], s5_b_im[i])
        bblk = jnp.concatenate([_block_diag(bb_re, gq), _block_diag(bb_im, gq)], axis=2).astype(BF16)
        c_re_t = jnp.swapaxes(s5_c_re[i], 1, 2)
        c_im_t = jnp.swapaxes(s5_c_im[i], 1, 2)
        cblk = jnp.concatenate([_block_diag(c_re_t, gq), _block_diag(-c_im_t, gq)], axis=1).astype(BF16)
        a_blk = _state_to_lanes(ab_re[None], ab_im[None], gq)
        wglu = s5_w_glu[i].astype(BF16)
        d_row = row(s5_d[i])

        if i == 0:
            h, gates = rms_gates(x[0], x[1], row(g_pre_mix[i]), w_ifs[i], b_ifs[i], n_heads)
        z = mm(h, w_main, BF16)

        ya_p, c1, n1, m1 = mlstm(z, gates, zero_c, zeros(bp, n_heads, dh), zeros(bp, 1, LANES), row(g_mh[i]),
                                 0, 0, bp, lp, n_heads, dh)
        ya_s, c2, n2, m2 = mlstm(z, gates, state_mlstm_c, state_mlstm_n[i], pad_m(state_mlstm_m[i]), row(g_mh[i]),
                                 i, np_rows, bs, ls, n_heads, dh)

        u_all = z[:, 4 * d_a:4 * d_a + d_b]
        u_p = _to_step_major(u_all[:np_rows], n_seg)
        u_s = _to_step_major(u_all[np_rows:], bs)
        wtot = a_blk.shape[1]
        (sfin,) = s5_scan(u_p, zeros(n_seg, wtot), a_blk, bblk, None, None, None, n_seg, False)
        hin = s5_chain(sfin, a_blk, seg_len, nq)
        yb_p, hfin_p = s5_scan(u_p, hin, a_blk, bblk, cblk, d_row, wglu, n_seg, True)
        yb_s, hfin_s = s5_scan(u_s, _state_to_lanes(state_s5_re[i], state_s5_im[i], gq), a_blk, bblk, cblk,
                               d_row, wglu, bs, True)
        yb_p, yb_s = _from_step_major(yb_p, n_seg), _from_step_major(yb_s, bs)
        re_p, im_p = _lanes_to_state(hfin_p[n_seg - 1:], g_b, p_b, gq)
        re_s, im_s = _lanes_to_state(hfin_s, g_b, p_b, gq)

        cw, cb, lg, lb = conv_w[i], row(conv_b[i]), row(conv_ln_g[i]), row(conv_ln_b[i])
        yc_p, cv_p = conv_module(z, zeros(bp, CACHE_PAD, d_c), cw, cb, lg, lb, 0, bp, lp, col_val, col_gate)
        yc_s, cv_s = conv_module(z, pad_cache(cache_conv[i]), cw, cb, lg, lb, np_rows, bs, ls, col_val, col_gate)

        x, h = mm_res([(ya_p, ya_s), (yb_p, yb_s), (yc_p, yc_s)], w_out_all, x, row(g_post_mix[i]), split="n",
                      layer=i, second="norm", g2=row(g_pre_ffn[i]), name="mm_out")

        f = mm_swiglu(h, w_gate_all, w_up_all, i, d_ff_pad)
        x, xb = mm_res([f], w_down_all, x, row(g_post_ffn[i]), split="k", layer=i, second="cast",
                       name="mm_down")

        pe = jnp.concatenate([p_prompt[i].reshape(np_rows, -1), p_sample[i].reshape(ns_rows, -1)], axis=0).astype(BF16)
        ple_args = dict(split="n", layer=i, pe=pe, wple=w_ple_all, name="mm_ple")
        if i + 1 < depth:
            x, h, gates = mm_res([xb], w_pgate_all, x, row(g_post_ple[i]), second="norm",
                                 g2=row(g_pre_mix[i + 1]), wif=w_ifs[i + 1], bif=b_ifs[i + 1], n_heads=n_heads,
                                 **ple_args)
        else:
            x = mm_res([xb], w_pgate_all, x, row(g_post_ple[i]), split_out=np_rows, **ple_args)

        first = CACHE_PAD - (width - 1)
        outs["pc"].append(c1); outs["pn"].append(n1); outs["pm"].append(m1[:, 0, :n_heads])
        outs["pre"].append(re_p); outs["pim"].append(im_p); outs["pcv"].append(cv_p[:, first:])
        outs["sc"].append(c2); outs["sn"].append(n2); outs["sm"].append(m2[:, 0, :n_heads])
        outs["sre"].append(re_s); outs["sim"].append(im_s); outs["scv"].append(cv_s[:, first:])

    st = lambda k: jnp.stack(outs[k])
    return (x[0].reshape(bp, lp, d_model), x[1].reshape(bs, ls, d_model),
            st("pc"), st("pn"), st("pm"), st("pre"), st("pim"), st("pcv"),
            st("sc"), st("sn"), st("sm"), st("sre"), st("sim"), st("scv"))
```

```python
import functools
import math

import jax
import jax.numpy as jnp
from jax import lax
from jax.experimental import pallas as pl
from jax.experimental.pallas import tpu as pltpu

F32 = jnp.float32
BF16 = jnp.bfloat16
EPS = 1e-6
LANES = 128
SUBLANES = 8
VMEM_LIMIT = 56 * 1024 * 1024
GATE_PAD = LANES
CACHE_PAD = 32


def _pick(n, pref):
    t = min(pref, n)
    while n % t:
        t //= 2
    return t


def _params(*sem):
    return pltpu.CompilerParams(dimension_semantics=sem, vmem_limit_bytes=VMEM_LIMIT)


def _rms(x, g):
    return x * lax.rsqrt(jnp.mean(x * x, axis=-1, keepdims=True) + EPS) * g


def _log_sigmoid(x):
    return jnp.minimum(x, 0.0) - jnp.log1p(jnp.exp(-jnp.abs(x)))


def _dot(a, b):
    return jnp.dot(a, b, preferred_element_type=F32)


def _gate_act(gt, n_heads):
    lane = lax.broadcasted_iota(jnp.int32, gt.shape, 1)
    return jnp.where(lane >= n_heads, _log_sigmoid(gt), gt)


def _rms_gates_kernel(x0_ref, x1_ref, g_ref, wif_ref, bias_ref, h_ref, gate_ref, *, n_heads, tiles0):
    x = jnp.where(pl.program_id(0) < tiles0, x0_ref[...], x1_ref[...])
    hb = _rms(x, g_ref[...]).astype(BF16)
    h_ref[...] = hb
    gate_ref[...] = _gate_act(_dot(hb, wif_ref[...]) + bias_ref[...], n_heads)


def rms_gates(x0, x1, g, wif, bias, n_heads):
    (n0, d), n1 = x0.shape, x1.shape[0]
    tm = _pick(math.gcd(n0, n1), 256)
    tiles0 = n0 // tm
    return pl.pallas_call(
        functools.partial(_rms_gates_kernel, n_heads=n_heads, tiles0=tiles0),
        grid=((n0 + n1) // tm,),
        in_specs=[pl.BlockSpec((tm, d), lambda i: (jnp.minimum(i, tiles0 - 1), 0)),
                  pl.BlockSpec((tm, d), lambda i: (jnp.maximum(i - tiles0, 0), 0)),
                  pl.BlockSpec((1, d), lambda i: (0, 0)),
                  pl.BlockSpec((d, GATE_PAD), lambda i: (0, 0)),
                  pl.BlockSpec((1, GATE_PAD), lambda i: (0, 0))],
        out_specs=[pl.BlockSpec((tm, d), lambda i: (i, 0)),
                   pl.BlockSpec((tm, GATE_PAD), lambda i: (i, 0))],
        out_shape=[jax.ShapeDtypeStruct((n0 + n1, d), BF16),
                   jax.ShapeDtypeStruct((n0 + n1, GATE_PAD), F32)],
        compiler_params=_params("parallel"), name="rms_gates",
    )(x0, x1, g, wif, bias)


def _mm_kernel(a_ref, b_ref, o_ref):
    o_ref[...] = _dot(a_ref[...], b_ref[...]).astype(o_ref.dtype)


def mm(a, b, layer, out_dtype):
    m, k = a.shape
    n = b.shape[2]
    tm, tn = _pick(m, 1024), _pick(n, 1024)
    return pl.pallas_call(
        _mm_kernel,
        grid=(m // tm, n // tn),
        in_specs=[pl.BlockSpec((tm, k), lambda i, j: (i, 0)),
                  pl.BlockSpec((None, k, tn), lambda i, j: (layer, 0, j))],
        out_specs=pl.BlockSpec((tm, tn), lambda i, j: (i, j)),
        out_shape=jax.ShapeDtypeStruct((m, n), out_dtype),
        compiler_params=_params("parallel", "arbitrary"), name="mm_in",
    )(a, b)


def _mm_swiglu_kernel(a_ref, bg_ref, bu_ref, o_ref, *, d_ff):
    j, tn = pl.program_id(1), o_ref.shape[1]
    ragged = (j + 1) * tn > d_ff

    def swiglu():
        a = a_ref[...]
        gate = _dot(a, bg_ref[...])
        return gate * jax.nn.sigmoid(gate) * _dot(a, bu_ref[...])

    @pl.when(jnp.logical_not(ragged))
    def _():
        o_ref[...] = swiglu().astype(o_ref.dtype)

    @pl.when(ragged)
    def _():
        f = swiglu()
        col = j * tn + lax.broadcasted_iota(jnp.int32, f.shape, 1)
        o_ref[...] = jnp.where(col < d_ff, f, 0.0).astype(o_ref.dtype)


def mm_swiglu(a, bg, bu, layer, n):
    m, k = a.shape
    d_ff = bg.shape[2]
    tm, tn = _pick(m, 1024), _pick(n, 512)
    wspec = pl.BlockSpec((None, k, tn), lambda i, j: (layer, 0, jnp.minimum(j, (d_ff - 1) // tn)))
    return pl.pallas_call(
        functools.partial(_mm_swiglu_kernel, d_ff=d_ff),
        grid=(m // tm, n // tn),
        in_specs=[pl.BlockSpec((tm, k), lambda i, j: (i, 0)), wspec, wspec],
        out_specs=pl.BlockSpec((tm, tn), lambda i, j: (i, j)),
        out_shape=jax.ShapeDtypeStruct((m, n), BF16),
        compiler_params=_params("parallel", "arbitrary"), name="mm_swiglu",
    )(a, bg, bu)


RES_ROW_TILE = 512
EPI_CHUNKS = 8
MXU_DEPTH = 256


def _mm_res_kernel(*refs, a_pairs, split, n_tiles, n_steps, nch, tn, rb, ple, second, gates, n_heads, n_x, n_xo,
                   p_chunks, p_tiles):
    refs = list(refs)
    a_refs = [(refs.pop(0), refs.pop(0)) if pair else refs.pop(0) for pair in a_pairs]
    b_ref = refs.pop(0)
    x_refs = [refs.pop(0) for _ in range(n_x)]
    g_ref = refs.pop(0)
    g2_ref = refs.pop(0) if second == "norm" else None
    pe_ref, wple_ref = (refs.pop(0), refs.pop(0)) if ple else (None, None)
    wif_ref, bif_ref = (refs.pop(0), refs.pop(0)) if gates else (None, None)
    xo_refs = [refs.pop(0) for _ in range(n_xo)]
    h_ref = refs.pop(0) if second else None
    gate_ref = refs.pop(0) if gates else None
    acc_refs = (refs.pop(0), refs.pop(0))
    i, s = pl.program_id(0), pl.program_id(1)
    in_first = jnp.where(i == 0, 0, (i - 1) * nch + jnp.minimum(s, nch - 1)) < p_chunks

    def load_a(a_ref):
        if isinstance(a_ref, tuple):
            return jnp.where(i < p_tiles, a_ref[0][...], a_ref[1][...])
        return a_ref[...]

    def matmul(acc_ref):
        if split == "n":
            part, k0 = None, 0
            for a_ref in a_refs:
                a = load_a(a_ref)
                kw = a.shape[1]
                d = _dot(a, b_ref[k0:k0 + kw, :])
                part = d if part is None else part + d
                k0 += kw
            if ple:
                part = _dot(pe_ref[...], wple_ref[...]) * jax.nn.sigmoid(part)
            acc_ref[:, pl.ds(pl.multiple_of(s * tn, tn), tn)] = part
        else:
            acc_ref[...] += _dot(load_a(a_refs[0]), b_ref[...])

    def epilogue(acc_ref):
        rows = pl.ds(pl.multiple_of(s * rb, rb), rb)
        f = acc_ref[rows, :]
        if split == "k":
            acc_ref[rows, :] = jnp.zeros_like(f)
        x = x_refs[0][...] if n_x == 1 else jnp.where(in_first, x_refs[0][...], x_refs[1][...])
        xn = x + _rms(f, g_ref[...])
        if n_xo == 1:
            xo_refs[0][...] = xn
        else:
            xo_refs[0][...] = jnp.where(in_first, xn, xo_refs[0][...])
            xo_refs[1][...] = jnp.where(in_first, xo_refs[1][...], xn)
        if second == "norm":
            hb = _rms(xn, g2_ref[...]).astype(BF16)
            h_ref[...] = hb
            if gates:
                gate_ref[...] = _gate_act(_dot(hb, wif_ref[...]) + bif_ref[...], n_heads)
        elif second == "cast":
            h_ref[...] = xn.astype(BF16)

    @pl.when((i == 0) & (s == 0))
    def _():
        for ref in list(acc_refs) + (xo_refs if n_xo == 2 else []):
            ref[...] = jnp.zeros_like(ref)

    for parity in range(2):
        mine = (i < n_tiles) & (lax.rem(i, 2) == parity)

        @pl.when(mine & (s < nch))
        def _(parity=parity):
            epilogue(acc_refs[1 - parity])
            matmul(acc_refs[parity])

        if n_steps > nch:
            @pl.when(mine & (s >= nch))
            def _(parity=parity):
                matmul(acc_refs[parity])

    @pl.when((i == n_tiles) & (s < nch))
    def _():
        epilogue(acc_refs[1 - n_tiles % 2])


def mm_res(a_parts, b, x, g, *, split, second=None, g2=None, pe=None, wple=None, wif=None, bif=None,
           n_heads=0, split_out=None, layer=0, name="mm_res"):
    xs = list(x) if isinstance(x, (tuple, list)) else [x]
    m = sum(v.shape[0] for v in xs)
    kdim, n = b.shape[1:]
    tm = _pick(m, RES_ROW_TILE)
    n_tiles, nch = m // tm, EPI_CHUNKS
    rb = tm // nch
    ple, gates = pe is not None, wif is not None
    once = pl.Buffered(1)
    const = lambda shape: pl.BlockSpec(shape, lambda i, s: (0, 0), pipeline_mode=once)
    tile = lambda i: jnp.minimum(i, n_tiles - 1)
    chunk = lambda i, s: jnp.where(i == 0, 0, (i - 1) * nch + jnp.minimum(s, nch - 1))
    chunk_spec = lambda w: pl.BlockSpec((rb, w), lambda i, s: (chunk(i, s), 0))
    a_pairs = [isinstance(a, (tuple, list)) for a in a_parts]
    first_rows = {v[0].shape[0] for v in [xs] + [a for a, pair in zip(a_parts, a_pairs) if pair] if len(v) == 2}
    first_rows |= {split_out} if split_out else set()
    assert len(first_rows) <= 1
    first_rows = first_rows.pop() if first_rows else 0
    assert first_rows % tm == 0
    p_chunks, p_tiles = first_rows // rb, first_rows // tm
    group_specs = [pl.BlockSpec((rb, n), lambda i, s: (jnp.minimum(chunk(i, s), p_chunks - 1), 0)),
                   pl.BlockSpec((rb, n), lambda i, s: (jnp.maximum(chunk(i, s) - p_chunks, 0), 0))]
    if split == "n":
        n_steps = nch
        tn = n // n_steps
        in_specs = []
        for a, pair in zip(a_parts, a_pairs):
            if pair:
                kw = a[0].shape[1]
                in_specs += [pl.BlockSpec((tm, kw), lambda i, s: (jnp.minimum(tile(i), p_tiles - 1), 0)),
                             pl.BlockSpec((tm, kw), lambda i, s: (jnp.maximum(tile(i) - p_tiles, 0), 0))]
            else:
                in_specs.append(pl.BlockSpec((tm, a.shape[1]), lambda i, s: (tile(i), 0)))
        in_specs.append(pl.BlockSpec((None, kdim, tn), lambda i, s: (layer, 0, s)))
    else:
        (a,) = a_parts
        assert not a_pairs[0]
        tn = n
        tk = max(t for t in range(MXU_DEPTH, kdim // nch + 1, MXU_DEPTH) if kdim % t == 0)
        n_steps = kdim // tk
        in_specs = [pl.BlockSpec((tm, tk), lambda i, s: (tile(i), s)),
                    pl.BlockSpec((None, tk, n), lambda i, s: (layer, s, 0))]
    in_specs += (group_specs if len(xs) == 2 else [chunk_spec(n)]) + [const((1, n))]
    args = [v for a, pair in zip(a_parts, a_pairs) for v in (a if pair else [a])] + [b] + xs + [g]
    if second == "norm":
        in_specs.append(const((1, n)))
        args.append(g2)
    if ple:
        assert split == "n"
        in_specs += [pl.BlockSpec((tm, pe.shape[1]), lambda i, s: (tile(i), 0)),
                     pl.BlockSpec((None, wple.shape[1], tn), lambda i, s: (layer, 0, s))]
        args += [pe, wple]
    if gates:
        in_specs += [const(wif.shape), const(bif.shape)]
        args += [wif, bif]
    if split_out:
        out_specs = list(group_specs)
        out_shape = [jax.ShapeDtypeStruct((split_out, n), F32), jax.ShapeDtypeStruct((m - split_out, n), F32)]
    else:
        out_specs, out_shape = [chunk_spec(n)], [jax.ShapeDtypeStruct((m, n), F32)]
    if second:
        out_specs.append(chunk_spec(n))
        out_shape.append(jax.ShapeDtypeStruct((m, n), BF16))
    if gates:
        out_specs.append(chunk_spec(GATE_PAD))
        out_shape.append(jax.ShapeDtypeStruct((m, GATE_PAD), F32))
    return pl.pallas_call(
        functools.partial(_mm_res_kernel, a_pairs=tuple(a_pairs), split=split, n_tiles=n_tiles, n_steps=n_steps,
                          nch=nch, tn=tn, rb=rb, ple=ple, second=second, gates=gates, n_heads=n_heads,
                          n_x=len(xs), n_xo=2 if split_out else 1, p_chunks=p_chunks, p_tiles=p_tiles),
        grid=(n_tiles + 1, n_steps),
        in_specs=in_specs, out_specs=out_specs, out_shape=out_shape,
        scratch_shapes=[pltpu.VMEM((tm, n), F32), pltpu.VMEM((tm, n), F32)],
        compiler_params=_params("arbitrary", "arbitrary"), name=name,
    )(*args)


def _split3(x):
    hi = x.astype(BF16)
    r1 = x - hi.astype(F32)
    mid = r1.astype(BF16)
    lo = (r1 - mid.astype(F32)).astype(BF16)
    return hi, mid, lo


def _mlstm_kernel(zq_ref, zk_ref, zv_ref, zo_ref, gate_ref, c0_ref, n0_ref, m0_ref,
                  gmh_ref, y_ref, c_ref, n_ref, m_ref, *, n_heads, dh, t):
    @pl.when(pl.program_id(1) == 0)
    def _():
        c_ref[...] = c0_ref[...]
        n_ref[...] = n0_ref[...]
        m_ref[...] = m0_ref[...]

    row = lax.broadcasted_iota(jnp.int32, (t, t), 0)
    col = lax.broadcasted_iota(jnp.int32, (t, t), 1)
    causal = row >= col
    tri = causal.astype(BF16)
    tri_t = (col >= row).astype(BF16)
    g_col = gate_ref[...]
    g_row = g_col.T
    cum_col = sum(_dot(tri, p) for p in _split3(g_col))
    cum_row = sum(_dot(p, tri_t) for p in _split3(g_row))
    scale = dh ** -0.5

    for h in range(n_heads):
        sl = slice(h * dh, (h + 1) * dh)
        q = zq_ref[:, sl]
        k = zk_ref[:, sl]
        v = zv_ref[:, sl]
        i_col = g_col[:, h:h + 1]
        i_row = g_row[h:h + 1, :]
        b_col = cum_col[:, n_heads + h:n_heads + h + 1]
        b_row = cum_row[n_heads + h:n_heads + h + 1, :]
        m_prev = m_ref[0, :, h:h + 1]
        c_prev = c_ref[0, h]
        n_prev = n_ref[0, h:h + 1, :]

        d = jnp.where(causal, b_col - b_row + i_row, -jnp.inf)
        inter = b_col + m_prev
        mt = jnp.maximum(inter, jnp.max(d, axis=1, keepdims=True))
        s = lax.dot_general(q, k, (((1,), (1,)), ((), ())), preferred_element_type=F32)
        s = s * scale * jnp.exp(d - mt)
        sc_t = jnp.exp(inter - mt)
        num = sc_t * _dot(q, c_prev.astype(BF16)) + _dot(s.astype(BF16), v)
        qn = jnp.sum(q.astype(F32) * n_prev, axis=1, keepdims=True)
        den = sc_t * qn + jnp.sum(s, axis=1, keepdims=True)
        hh = num / jnp.maximum(jnp.abs(den), jnp.exp(-mt))

        b_last = b_col[t - 1:t, :]
        m_new = mt[t - 1:t, :]
        wl = jnp.exp(b_last - b_col + i_col - m_new)
        sc = jnp.exp(b_last + m_prev - m_new)
        kw = k.astype(F32) * wl
        kv = lax.dot_general(kw.astype(BF16), v, (((0,), (0,)), ((), ())), preferred_element_type=F32)
        c_ref[0, h] = sc * c_prev + kv * scale
        n_ref[0, h:h + 1, :] = sc * n_prev + jnp.sum(kw, axis=0, keepdims=True) * scale
        m_ref[0, :, h:h + 1] = m_new

        hn = _rms(hh, gmh_ref[:, sl])
        y_ref[:, sl] = (hn * jax.nn.sigmoid(zo_ref[:, sl].astype(F32))).astype(BF16)


def mlstm(z, gates, c0, n0, m0, gmh, layer, row_off, bsz, seq, n_heads, dh):
    d_a = n_heads * dh
    t = min(256, seq)
    nc = seq // t
    r0 = row_off // t
    rmap = lambda b, c: r0 + b * nc + c
    zspec = lambda j: pl.BlockSpec((t, d_a), lambda b, c, j=j: (rmap(b, c), j))
    return pl.pallas_call(
        functools.partial(_mlstm_kernel, n_heads=n_heads, dh=dh, t=t),
        grid=(bsz, nc),
        in_specs=[zspec(0), zspec(1), zspec(2), zspec(3),
                  pl.BlockSpec((t, GATE_PAD), lambda b, c: (rmap(b, c), 0)),
                  pl.BlockSpec((None, 1, n_heads, dh, dh), lambda b, c: (layer, b, 0, 0, 0)),
                  pl.BlockSpec((1, n_heads, dh), lambda b, c: (b, 0, 0)),
                  pl.BlockSpec((1, 1, LANES), lambda b, c: (b, 0, 0)),
                  pl.BlockSpec((1, d_a), lambda b, c: (0, 0))],
        out_specs=[pl.BlockSpec((t, d_a), lambda b, c: (b * nc + c, 0)),
                   pl.BlockSpec((1, n_heads, dh, dh), lambda b, c: (b, 0, 0, 0)),
                   pl.BlockSpec((1, n_heads, dh), lambda b, c: (b, 0, 0)),
                   pl.BlockSpec((1, 1, LANES), lambda b, c: (b, 0, 0))],
        out_shape=[jax.ShapeDtypeStruct((bsz * seq, d_a), BF16),
                   jax.ShapeDtypeStruct((bsz, n_heads, dh, dh), F32),
                   jax.ShapeDtypeStruct((bsz, n_heads, dh), F32),
                   jax.ShapeDtypeStruct((bsz, 1, LANES), F32)],
        compiler_params=_params("parallel", "arbitrary"), name="mlstm",
    )(z, z, z, z, gates, c0, n0, m0, gmh)


S5_ROWS = 512


def _s5_disc_kernel(lr_ref, li_ref, ldt_ref, br_ref, bi_ref, abr_ref, abi_ref, bbr_ref, bbi_ref):
    dt = jnp.exp(ldt_ref[...])
    lr = lr_ref[...]
    li = li_ref[...]
    mag = jnp.exp(lr * dt)
    ab_re = mag * jnp.cos(li * dt)
    ab_im = mag * jnp.sin(li * dt)
    den = lr * lr + li * li
    nr = ab_re - 1.0
    coef_re = (nr * lr + ab_im * li) / den
    coef_im = (ab_im * lr - nr * li) / den
    br = br_ref[...]
    bi = bi_ref[...]
    abr_ref[...] = ab_re
    abi_ref[...] = ab_im
    bbr_ref[...] = coef_re * br - coef_im * bi
    bbi_ref[...] = coef_re * bi + coef_im * br


def s5_discretise(lam_re, lam_im, log_dt, b_re, b_im):
    g, p, j = b_re.shape
    rep = lambda a: jnp.repeat(a, j, axis=0)
    br_t = jnp.swapaxes(b_re, 1, 2).reshape(g * j, p)
    bi_t = jnp.swapaxes(b_im, 1, 2).reshape(g * j, p)
    shp = jax.ShapeDtypeStruct((g * j, p), F32)
    abr, abi, bbr, bbi = pl.pallas_call(
        _s5_disc_kernel, out_shape=[shp, shp, shp, shp], name="s5_disc",
    )(rep(lam_re), rep(lam_im), rep(log_dt[:, None]), br_t, bi_t)
    first = lambda a: a.reshape(g, j, p)[:, 0]
    return first(abr), first(abi), bbr.reshape(g, j, p), bbi.reshape(g, j, p)


def _s5_scan_kernel(*refs, s_rows, tc, nq, gw, slab, emit_y):
    if emit_y:
        u_ref, h0_ref, a_ref, bblk_ref, cblk_ref, d_ref, wglu_ref, y_ref, hout_ref, bu_scr = refs
    else:
        u_ref, h0_ref, a_ref, bblk_ref, hout_ref, bu_scr = refs

    @pl.when(pl.program_id(0) == 0)
    def _():
        hout_ref[...] = h0_ref[...]

    u = u_ref[...]
    kq = u.shape[1] // nq
    for q in range(nq):
        bu_scr[:, q * 2 * gw:(q + 1) * 2 * gw] = _dot(u[:, q * kq:(q + 1) * kq], bblk_ref[q])

    for q in range(nq):
        for lo in range(0, gw, slab):
            re = slice(q * 2 * gw + lo, q * 2 * gw + lo + slab)
            im = slice(q * 2 * gw + gw + lo, q * 2 * gw + gw + lo + slab)
            ar = a_ref[:, re]
            ai = a_ref[:, im]

            def body(step, carry, re=re, im=im, ar=ar, ai=ai):
                hr, hi = carry
                rows = pl.ds(pl.multiple_of(step * s_rows, s_rows), s_rows)
                nr = ar * hr - ai * hi + bu_scr[rows, re]
                ni = ar * hi + ai * hr + bu_scr[rows, im]
                if emit_y:
                    bu_scr[rows, re] = nr
                    bu_scr[rows, im] = ni
                return nr, ni

            hr, hi = lax.fori_loop(0, tc, body, (hout_ref[:, re], hout_ref[:, im]), unroll=2)
            hout_ref[:, re] = hr
            hout_ref[:, im] = hi

    if emit_y:
        ys = [_dot(bu_scr[:, q * 2 * gw:(q + 1) * 2 * gw].astype(BF16), cblk_ref[q]) for q in range(nq)]
        y = jnp.concatenate(ys, axis=1) + d_ref[...] * u.astype(F32)
        zg = 0.5 * y * (1.0 + jnp.tanh(math.sqrt(2.0 / math.pi) * (y + 0.044715 * (y * y * y))))
        y_ref[...] = (zg * jax.nn.sigmoid(_dot(zg.astype(BF16), wglu_ref[...]))).astype(BF16)


def s5_scan(u_perm, h0, a_blk, bblk, cblk, d, wglu, s_rows, emit_y):
    rows, d_b = u_perm.shape
    nq, kq, w2 = bblk.shape
    gw = w2 // 2
    steps = rows // s_rows
    tc = _pick(steps, max(1, S5_ROWS // s_rows))
    rb = tc * s_rows
    wtot = nq * w2
    const2 = lambda c: (0, 0)
    in_specs = [pl.BlockSpec((rb, d_b), lambda c: (c, 0)),
                pl.BlockSpec((s_rows, wtot), const2),
                pl.BlockSpec((1, wtot), const2),
                pl.BlockSpec((nq, kq, w2), lambda c: (0, 0, 0))]
    args = [u_perm, h0, a_blk, bblk]
    out_specs = [pl.BlockSpec((s_rows, wtot), const2)]
    out_shape = [jax.ShapeDtypeStruct((s_rows, wtot), F32)]
    if emit_y:
        in_specs += [pl.BlockSpec((nq, w2, kq), lambda c: (0, 0, 0)),
                     pl.BlockSpec((1, d_b), const2),
                     pl.BlockSpec((d_b, d_b), const2)]
        args += [cblk, d, wglu]
        out_specs = [pl.BlockSpec((rb, d_b), lambda c: (c, 0))] + out_specs
        out_shape = [jax.ShapeDtypeStruct((rows, d_b), BF16)] + out_shape
    return pl.pallas_call(
        functools.partial(_s5_scan_kernel, s_rows=s_rows, tc=tc, nq=nq, gw=gw,
                          slab=min(512, gw), emit_y=emit_y),
        grid=(steps // tc,),
        in_specs=in_specs, out_specs=out_specs, out_shape=out_shape,
        scratch_shapes=[pltpu.VMEM((rb, wtot), F32)],
        compiler_params=_params("arbitrary"), name="s5_scan" if emit_y else "s5_ends",
    )(*args)


def _s5_chain_kernel(sfin_ref, a_ref, hin_ref, *, seg_len, n_seg, nq, gw):
    for q in range(nq):
        re = slice(q * 2 * gw, q * 2 * gw + gw)
        im = slice(q * 2 * gw + gw, (q + 1) * 2 * gw)
        br, bi = a_ref[:, re], a_ref[:, im]
        pr, pi = jnp.ones_like(br), jnp.zeros_like(br)
        e = seg_len
        while e:
            if e & 1:
                pr, pi = pr * br - pi * bi, pr * bi + pi * br
            br, bi = br * br - bi * bi, 2.0 * br * bi
            e >>= 1
        hr, hi = jnp.zeros_like(pr), jnp.zeros_like(pr)
        for s in range(n_seg):
            hin_ref[s:s + 1, re] = hr
            hin_ref[s:s + 1, im] = hi
            hr, hi = (pr * hr - pi * hi + sfin_ref[s:s + 1, re],
                      pr * hi + pi * hr + sfin_ref[s:s + 1, im])


def s5_chain(sfin, a_blk, seg_len, nq):
    n_seg, wtot = sfin.shape
    return pl.pallas_call(
        functools.partial(_s5_chain_kernel, seg_len=seg_len, n_seg=n_seg, nq=nq, gw=wtot // nq // 2),
        out_shape=jax.ShapeDtypeStruct((n_seg, wtot), F32), name="s5_chain",
    )(sfin, a_blk)


def _conv_kernel(val_ref, gate_ref, cache_ref, w_ref, b_ref, lng_ref, lnb_ref, y_ref, cout_ref, xp_scr, xs_scr,
                 *, t, width):
    c = pl.program_id(1)

    @pl.when(c == 0)
    def _():
        xp_scr[0:CACHE_PAD, :] = cache_ref[0]

    @pl.when(c > 0)
    def _():
        xp_scr[0:CACHE_PAD, :] = xp_scr[t:t + CACHE_PAD, :]

    xp_scr[CACHE_PAD:CACHE_PAD + t, :] = val_ref[...].astype(F32) * jax.nn.sigmoid(gate_ref[...].astype(F32))
    first = CACHE_PAD - (width - 1)
    acc = None
    for r in range(min(SUBLANES, width)):
        taps = range(r, width, SUBLANES)
        rows = t + SUBLANES * (len(taps) - 1)
        xs_scr[0:rows, :] = xp_scr[first + r:first + r + rows, :]
        for a, j in enumerate(taps):
            term = w_ref[j:j + 1, :] * xs_scr[SUBLANES * a:SUBLANES * a + t, :]
            acc = term if acc is None else acc + term
    y = acc + b_ref[...]
    mu = jnp.mean(y, axis=-1, keepdims=True)
    yc = y - mu
    var = jnp.mean(yc * yc, axis=-1, keepdims=True)
    y = yc * lax.rsqrt(var + EPS) * lng_ref[...] + lnb_ref[...]
    y_ref[...] = (y * jax.nn.sigmoid(y)).astype(BF16)
    cout_ref[0] = xp_scr[t:t + CACHE_PAD, :]


def conv_module(z, cache, w, b, lng, lnb, row_off, bsz, seq, col_val, col_gate):
    width, d_c = w.shape
    t = min(256, seq)
    assert t >= CACHE_PAD
    nc = seq // t
    r0 = row_off // t
    rmap = lambda bb, c: r0 + bb * nc + c
    vec = pl.BlockSpec((1, d_c), lambda bb, c: (0, 0))
    return pl.pallas_call(
        functools.partial(_conv_kernel, t=t, width=width),
        grid=(bsz, nc),
        in_specs=[pl.BlockSpec((t, d_c), lambda bb, c: (rmap(bb, c), col_val)),
                  pl.BlockSpec((t, d_c), lambda bb, c: (rmap(bb, c), col_gate)),
                  pl.BlockSpec((1, CACHE_PAD, d_c), lambda bb, c: (bb, 0, 0)),
                  pl.BlockSpec((width, d_c), lambda bb, c: (0, 0)),
                  vec, vec, vec],
        out_specs=[pl.BlockSpec((t, d_c), lambda bb, c: (bb * nc + c, 0)),
                   pl.BlockSpec((1, CACHE_PAD, d_c), lambda bb, c: (bb, 0, 0))],
        out_shape=[jax.ShapeDtypeStruct((bsz * seq, d_c), BF16),
                   jax.ShapeDtypeStruct((bsz, CACHE_PAD, d_c), F32)],
        scratch_shapes=[pltpu.VMEM((CACHE_PAD + t, d_c), F32), pltpu.VMEM((CACHE_PAD + t, d_c), F32)],
        compiler_params=_params("parallel", "arbitrary"), name="conv",
    )(z, z, cache, w, b, lng, lnb)


def _block_diag(w, gq):
    g, r, c = w.shape
    eye = jnp.eye(gq, dtype=w.dtype)
    w = w.reshape(g // gq, gq, r, c)
    return jnp.einsum("qgrc,gh->qgrhc", w, eye).reshape(g // gq, gq * r, gq * c)


def _state_to_lanes(re, im, gq):
    s, g, p = re.shape
    both = jnp.stack([re.reshape(s, g // gq, gq * p), im.reshape(s, g // gq, gq * p)], axis=2)
    return both.reshape(s, -1)


def _lanes_to_state(h, g, p, gq):
    s = h.shape[0]
    both = h.reshape(s, g // gq, 2, gq * p)
    return both[:, :, 0].reshape(s, g, p), both[:, :, 1].reshape(s, g, p)


def _to_step_major(x, n_seq):
    rows, d = x.shape
    return x.reshape(n_seq, rows // n_seq, d).swapaxes(0, 1).reshape(rows, d)


def _from_step_major(x, n_seq):
    rows, d = x.shape
    return x.reshape(rows // n_seq, n_seq, d).swapaxes(0, 1).reshape(rows, d)


PROMPT_SEGMENTS = 32


def kernel(x_prompt, x_sample, p_prompt, p_sample, state_mlstm_c, state_mlstm_n, state_mlstm_m, state_s5_re, state_s5_im, cache_conv, g_pre_mix, w_in, b_igate, b_fgate, g_mh, s5_lam_re, s5_lam_im, s5_log_dt, s5_b_re, s5_b_im, s5_c_re, s5_c_im, s5_d, s5_w_glu, conv_w, conv_b, conv_ln_g, conv_ln_b, w_out, g_post_mix, g_pre_ffn, w_ffn_gate, w_ffn_up, w_ffn_down, g_post_ffn, w_ple, w_ple_gate, g_post_ple):
    depth = w_in.shape[0]
    bp, lp, d_model = x_prompt.shape
    bs, ls, _ = x_sample.shape
    n_heads = b_igate.shape[1]
    d_a = g_mh.shape[1]
    dh = d_a // n_heads
    g_b, p_b, j_b = s5_b_re.shape[1:]
    d_b = g_b * j_b
    width, d_c = conv_w.shape[1:]
    d_ff = w_ffn_gate.shape[2]
    assert bp == 1 and 2 * n_heads <= GATE_PAD and d_b == d_c and d_a % d_b == 0
    gq = 256 // j_b
    nq = g_b // gq
    np_rows, ns_rows = bp * lp, bs * ls
    n_seg = min(PROMPT_SEGMENTS, lp // 8)
    seg_len = lp // n_seg
    col_u = 4 * d_a // d_b
    col_val, col_gate = col_u + 1, col_u + 2
    d_ff_pad = -(-d_ff // 1024) * 1024

    x = (x_prompt.reshape(np_rows, d_model), x_sample.reshape(ns_rows, d_model))
    row = lambda v: v.reshape(1, -1).astype(F32)
    zeros = lambda *s: jnp.zeros(s, F32)
    pad_m = lambda m: jnp.pad(m, ((0, 0), (0, LANES - n_heads)))[:, None, :]
    pad_cache = lambda c: jnp.pad(c, ((0, 0), (CACHE_PAD - (width - 1), 0), (0, 0)))

    n_qkvo = 4 * d_a
    w_ifs = [jnp.pad(w_in[i][:, n_qkvo:n_qkvo + 2 * n_heads], ((0, 0), (0, GATE_PAD - 2 * n_heads))).astype(BF16)
             for i in range(depth)]
    b_ifs = [jnp.pad(jnp.concatenate([b_igate[i], b_fgate[i]]), (0, GATE_PAD - 2 * n_heads)).reshape(1, GATE_PAD)
             for i in range(depth)]
    w_main_all = jnp.concatenate([w_in[:, :, :n_qkvo], w_in[:, :, n_qkvo + 2 * n_heads:]], axis=2).astype(BF16)
    w_gate_all, w_up_all = w_ffn_gate.astype(BF16), w_ffn_up.astype(BF16)
    w_down_all = jnp.pad(w_ffn_down.astype(BF16), ((0, 0), (0, d_ff_pad - d_ff), (0, 0)))
    w_out_all, w_pgate_all, w_ple_all = w_out.astype(BF16), w_ple_gate.astype(BF16), w_ple.astype(BF16)
    zero_c = zeros(1, bp, n_heads, dh, dh)
    outs = {k: [] for k in ("pc", "sc", "pn", "pm", "pre", "pim", "pcv", "sn", "sm", "sre", "sim", "scv")}
    for i in range(depth):
        ab_re, ab_im, bb_re, bb_im = s5_discretise(s5_lam_re[i], s5_lam_im[i], s5_log_dt[i], s5_b_re[i], s5_b_im[i])
        bblk = jnp.concatenate([_block_diag(bb_re, gq), _block_diag(bb_im, gq)], axis=2).astype(BF16)
        c_re_t = jnp.swapaxes(s5_c_re[i], 1, 2)
        c_im_t = jnp.swapaxes(s5_c_im[i], 1, 2)
        cblk = jnp.concatenate([_block_diag(c_re_t, gq), _block_diag(-c_im_t, gq)], axis=1).astype(BF16)
        a_blk = _state_to_lanes(ab_re[None], ab_im[None], gq)
        wglu = s5_w_glu[i].astype(BF16)
        d_row = row(s5_d[i])

        if i == 0:
            h, gates = rms_gates(x[0], x[1], row(g_pre_mix[i]), w_ifs[i], b_ifs[i], n_heads)
        z = mm(h, w_main_all, i, BF16)

        ya_p, c1, n1, m1 = mlstm(z, gates, zero_c, zeros(bp, n_heads, dh), zeros(bp, 1, LANES), row(g_mh[i]),
                                 0, 0, bp, lp, n_heads, dh)
        ya_s, c2, n2, m2 = mlstm(z, gates, state_mlstm_c, state_mlstm_n[i], pad_m(state_mlstm_m[i]), row(g_mh[i]),
                                 i, np_rows, bs, ls, n_heads, dh)

        u_all = z[:, 4 * d_a:4 * d_a + d_b]
        u_p = _to_step_major(u_all[:np_rows], n_seg)
        u_s = _to_step_major(u_all[np_rows:], bs)
        wtot = a_blk.shape[1]
        (sfin,) = s5_scan(u_p, zeros(n_seg, wtot), a_blk, bblk, None, None, None, n_seg, False)
        hin = s5_chain(sfin, a_blk, seg_len, nq)
        yb_p, hfin_p = s5_scan(u_p, hin, a_blk, bblk, cblk, d_row, wglu, n_seg, True)
        yb_s, hfin_s = s5_scan(u_s, _state_to_lanes(state_s5_re[i], state_s5_im[i], gq), a_blk, bblk, cblk,
                               d_row, wglu, bs, True)
        yb_p, yb_s = _from_step_major(yb_p, n_seg), _from_step_major(yb_s, bs)
        re_p, im_p = _lanes_to_state(hfin_p[n_seg - 1:], g_b, p_b, gq)
        re_s, im_s = _lanes_to_state(hfin_s, g_b, p_b, gq)

        cw, cb, lg, lb = conv_w[i], row(conv_b[i]), row(conv_ln_g[i]), row(conv_ln_b[i])
        yc_p, cv_p = conv_module(z, zeros(bp, CACHE_PAD, d_c), cw, cb, lg, lb, 0, bp, lp, col_val, col_gate)
        yc_s, cv_s = conv_module(z, pad_cache(cache_conv[i]), cw, cb, lg, lb, np_rows, bs, ls, col_val, col_gate)

        x, h = mm_res([(ya_p, ya_s), (yb_p, yb_s), (yc_p, yc_s)], w_out_all, x, row(g_post_mix[i]), split="n",
                      layer=i, second="norm", g2=row(g_pre_ffn[i]), name="mm_out")

        f = mm_swiglu(h, w_gate_all, w_up_all, i, d_ff_pad)
        x, xb = mm_res([f], w_down_all, x, row(g_post_ffn[i]), split="k", layer=i, second="cast",
                       name="mm_down")

        pe = jnp.concatenate([p_prompt[i].reshape(np_rows, -1), p_sample[i].reshape(ns_rows, -1)], axis=0).astype(BF16)
        ple_args = dict(split="n", layer=i, pe=pe, wple=w_ple_all, name="mm_ple")
        if i + 1 < depth:
            x, h, gates = mm_res([xb], w_pgate_all, x, row(g_post_ple[i]), second="norm",
                                 g2=row(g_pre_mix[i + 1]), wif=w_ifs[i + 1], bif=b_ifs[i + 1], n_heads=n_heads,
                                 **ple_args)
        else:
            x = mm_res([xb], w_pgate_all, x, row(g_post_ple[i]), split_out=np_rows, **ple_args)

        first = CACHE_PAD - (width - 1)
        outs["pc"].append(c1); outs["pn"].append(n1); outs["pm"].append(m1[:, 0, :n_heads])
        outs["pre"].append(re_p); outs["pim"].append(im_p); outs["pcv"].append(cv_p[:, first:])
        outs["sc"].append(c2); outs["sn"].append(n2); outs["sm"].append(m2[:, 0, :n_heads])
        outs["sre"].append(re_s); outs["sim"].append(im_s); outs["scv"].append(cv_s[:, first:])

    st = lambda k: jnp.stack(outs[k])
    return (x[0].reshape(bp, lp, d_model), x[1].reshape(bs, ls, d_model),
            st("pc"), st("pn"), st("pm"), st("pre"), st("pim"), st("pcv"),
            st("sc"), st("sn"), st("sm"), st("sre"), st("sim"), st("scv"))
```

```python
import functools
import math

import jax
import jax.numpy as jnp
from jax import lax
from jax.experimental import pallas as pl
from jax.experimental.pallas import tpu as pltpu

F32 = jnp.float32
BF16 = jnp.bfloat16
EPS = 1e-6
LANES = 128
SUBLANES = 8
VMEM_LIMIT = 56 * 1024 * 1024
GATE_PAD = LANES
CACHE_PAD = 32


def _pick(n, pref):
    t = min(pref, n)
    while n % t:
        t //= 2
    return t


def _params(*sem):
    return pltpu.CompilerParams(dimension_semantics=sem, vmem_limit_bytes=VMEM_LIMIT)


def _rms(x, g):
    return x * lax.rsqrt(jnp.mean(x * x, axis=-1, keepdims=True) + EPS) * g


def _log_sigmoid(x):
    return jnp.minimum(x, 0.0) - jnp.log1p(jnp.exp(-jnp.abs(x)))


def _dot(a, b):
    return jnp.dot(a, b, preferred_element_type=F32)


def _gate_act(gt, n_heads):
    lane = lax.broadcasted_iota(jnp.int32, gt.shape, 1)
    return jnp.where(lane >= n_heads, _log_sigmoid(gt), gt)


def _rms_gates_kernel(x0_ref, x1_ref, g_ref, wif_ref, bias_ref, h_ref, gate_ref, *, n_heads, tiles0):
    x = jnp.where(pl.program_id(0) < tiles0, x0_ref[...], x1_ref[...])
    hb = _rms(x, g_ref[...]).astype(BF16)
    h_ref[...] = hb
    gate_ref[...] = _gate_act(_dot(hb, wif_ref[...]) + bias_ref[...], n_heads)


def rms_gates(x0, x1, g, wif, bias, n_heads):
    (n0, d), n1 = x0.shape, x1.shape[0]
    tm = _pick(math.gcd(n0, n1), 256)
    tiles0 = n0 // tm
    return pl.pallas_call(
        functools.partial(_rms_gates_kernel, n_heads=n_heads, tiles0=tiles0),
        grid=((n0 + n1) // tm,),
        in_specs=[pl.BlockSpec((tm, d), lambda i: (jnp.minimum(i, tiles0 - 1), 0)),
                  pl.BlockSpec((tm, d), lambda i: (jnp.maximum(i - tiles0, 0), 0)),
                  pl.BlockSpec((1, d), lambda i: (0, 0)),
                  pl.BlockSpec((d, GATE_PAD), lambda i: (0, 0)),
                  pl.BlockSpec((1, GATE_PAD), lambda i: (0, 0))],
        out_specs=[pl.BlockSpec((tm, d), lambda i: (i, 0)),
                   pl.BlockSpec((tm, GATE_PAD), lambda i: (i, 0))],
        out_shape=[jax.ShapeDtypeStruct((n0 + n1, d), BF16),
                   jax.ShapeDtypeStruct((n0 + n1, GATE_PAD), F32)],
        compiler_params=_params("parallel"), name="rms_gates",
    )(x0, x1, g, wif, bias)


def _mm_kernel(a_ref, b_ref, o_ref):
    o_ref[...] = _dot(a_ref[...], b_ref[...]).astype(o_ref.dtype)


def mm(a, b, out_dtype):
    m, k = a.shape
    n = b.shape[1]
    tm, tn = _pick(m, 1024), _pick(n, 1024)
    return pl.pallas_call(
        _mm_kernel,
        grid=(m // tm, n // tn),
        in_specs=[pl.BlockSpec((tm, k), lambda i, j: (i, 0)),
                  pl.BlockSpec((k, tn), lambda i, j: (0, j))],
        out_specs=pl.BlockSpec((tm, tn), lambda i, j: (i, j)),
        out_shape=jax.ShapeDtypeStruct((m, n), out_dtype),
        compiler_params=_params("parallel", "arbitrary"), name="mm_in",
    )(a, b)


def _mm_swiglu_kernel(a_ref, bg_ref, bu_ref, o_ref, *, d_ff):
    j, tn = pl.program_id(1), o_ref.shape[1]
    ragged = (j + 1) * tn > d_ff

    def swiglu():
        a = a_ref[...]
        gate = _dot(a, bg_ref[...])
        return gate * jax.nn.sigmoid(gate) * _dot(a, bu_ref[...])

    @pl.when(jnp.logical_not(ragged))
    def _():
        o_ref[...] = swiglu().astype(o_ref.dtype)

    @pl.when(ragged)
    def _():
        f = swiglu()
        col = j * tn + lax.broadcasted_iota(jnp.int32, f.shape, 1)
        o_ref[...] = jnp.where(col < d_ff, f, 0.0).astype(o_ref.dtype)


def mm_swiglu(a, bg, bu, layer, n):
    m, k = a.shape
    d_ff = bg.shape[2]
    tm, tn = _pick(m, 1024), _pick(n, 512)
    wspec = pl.BlockSpec((None, k, tn), lambda i, j: (layer, 0, jnp.minimum(j, (d_ff - 1) // tn)))
    return pl.pallas_call(
        functools.partial(_mm_swiglu_kernel, d_ff=d_ff),
        grid=(m // tm, n // tn),
        in_specs=[pl.BlockSpec((tm, k), lambda i, j: (i, 0)), wspec, wspec],
        out_specs=pl.BlockSpec((tm, tn), lambda i, j: (i, j)),
        out_shape=jax.ShapeDtypeStruct((m, n), BF16),
        compiler_params=_params("parallel", "arbitrary"), name="mm_swiglu",
    )(a, bg, bu)


RES_ROW_TILE = 512
EPI_CHUNKS = 8
MXU_DEPTH = 256


def _mm_res_kernel(*refs, a_pairs, split, n_tiles, n_steps, nch, tn, rb, ple, second, gates, n_heads, n_x, n_xo,
                   p_chunks, p_tiles, last_rows):
    refs = list(refs)
    a_refs = [(refs.pop(0), refs.pop(0)) if pair else refs.pop(0) for pair in a_pairs]
    b_ref = refs.pop(0)
    x_refs = [refs.pop(0) for _ in range(n_x)]
    g_ref = refs.pop(0)
    g2_ref = refs.pop(0) if second == "norm" else None
    pe_ref, wple_ref = (refs.pop(0), refs.pop(0)) if ple else (None, None)
    wif_ref, bif_ref = (refs.pop(0), refs.pop(0)) if gates else (None, None)
    xo_refs = [refs.pop(0) for _ in range(n_xo)]
    h_ref = refs.pop(0) if second else None
    gate_ref = refs.pop(0) if gates else None
    acc_refs = (refs.pop(0), refs.pop(0))
    i, s = pl.program_id(0), pl.program_id(1)
    in_first = jnp.where(i == 0, 0, (i - 1) * nch + jnp.minimum(s, nch - 1)) < p_chunks

    def load_a(a_ref):
        if isinstance(a_ref, tuple):
            return jnp.where(i < p_tiles, a_ref[0][...], a_ref[1][...])
        return a_ref[...]

    def matmul(acc_ref, ragged=False):
        if split == "n":
            part, k0 = None, 0
            for a_ref in a_refs:
                a = load_a(a_ref)
                kw = a.shape[1]
                d = _dot(a, b_ref[k0:k0 + kw, :])
                part = d if part is None else part + d
                k0 += kw
            if ple:
                part = _dot(pe_ref[...], wple_ref[...]) * jax.nn.sigmoid(part)
            acc_ref[:, pl.ds(pl.multiple_of(s * tn, tn), tn)] = part
        else:
            b = b_ref[...]
            if ragged:
                b = jnp.where(lax.broadcasted_iota(jnp.int32, b.shape, 0) < last_rows, b, jnp.zeros_like(b))
            acc_ref[...] += _dot(load_a(a_refs[0]), b)

    def epilogue(acc_ref):
        rows = pl.ds(pl.multiple_of(s * rb, rb), rb)
        f = acc_ref[rows, :]
        if split == "k":
            acc_ref[rows, :] = jnp.zeros_like(f)
        x = x_refs[0][...] if n_x == 1 else jnp.where(in_first, x_refs[0][...], x_refs[1][...])
        xn = x + _rms(f, g_ref[...])
        if n_xo == 1:
            xo_refs[0][...] = xn
        else:
            xo_refs[0][...] = jnp.where(in_first, xn, xo_refs[0][...])
            xo_refs[1][...] = jnp.where(in_first, xo_refs[1][...], xn)
        if second == "norm":
            hb = _rms(xn, g2_ref[...]).astype(BF16)
            h_ref[...] = hb
            if gates:
                gate_ref[...] = _gate_act(_dot(hb, wif_ref[...]) + bif_ref[...], n_heads)
        elif second == "cast":
            h_ref[...] = xn.astype(BF16)

    @pl.when((i == 0) & (s == 0))
    def _():
        for ref in list(acc_refs) + (xo_refs if n_xo == 2 else []):
            ref[...] = jnp.zeros_like(ref)

    for parity in range(2):
        mine = (i < n_tiles) & (lax.rem(i, 2) == parity)

        @pl.when(mine & (s < nch))
        def _(parity=parity):
            epilogue(acc_refs[1 - parity])
            matmul(acc_refs[parity])

        if n_steps > nch:
            whole = n_steps if last_rows is None else n_steps - 1

            @pl.when(mine & (s >= nch) & (s < whole))
            def _(parity=parity):
                matmul(acc_refs[parity])

            if whole < n_steps:
                @pl.when(mine & (s == whole))
                def _(parity=parity):
                    matmul(acc_refs[parity], ragged=True)

    @pl.when((i == n_tiles) & (s < nch))
    def _():
        epilogue(acc_refs[1 - n_tiles % 2])


def mm_res(a_parts, b, x, g, *, split, second=None, g2=None, pe=None, wple=None, wif=None, bif=None,
           n_heads=0, split_out=None, layer=0, name="mm_res"):
    xs = list(x) if isinstance(x, (tuple, list)) else [x]
    m = sum(v.shape[0] for v in xs)
    kdim, n = b.shape[1:]
    last_rows = None
    tm = _pick(m, RES_ROW_TILE)
    n_tiles, nch = m // tm, EPI_CHUNKS
    rb = tm // nch
    ple, gates = pe is not None, wif is not None
    once = pl.Buffered(1)
    const = lambda shape: pl.BlockSpec(shape, lambda i, s: (0, 0), pipeline_mode=once)
    tile = lambda i: jnp.minimum(i, n_tiles - 1)
    chunk = lambda i, s: jnp.where(i == 0, 0, (i - 1) * nch + jnp.minimum(s, nch - 1))
    chunk_spec = lambda w: pl.BlockSpec((rb, w), lambda i, s: (chunk(i, s), 0))
    a_pairs = [isinstance(a, (tuple, list)) for a in a_parts]
    first_rows = {v[0].shape[0] for v in [xs] + [a for a, pair in zip(a_parts, a_pairs) if pair] if len(v) == 2}
    first_rows |= {split_out} if split_out else set()
    assert len(first_rows) <= 1
    first_rows = first_rows.pop() if first_rows else 0
    assert first_rows % tm == 0
    p_chunks, p_tiles = first_rows // rb, first_rows // tm
    group_specs = [pl.BlockSpec((rb, n), lambda i, s: (jnp.minimum(chunk(i, s), p_chunks - 1), 0)),
                   pl.BlockSpec((rb, n), lambda i, s: (jnp.maximum(chunk(i, s) - p_chunks, 0), 0))]
    if split == "n":
        n_steps = nch
        tn = n // n_steps
        in_specs = []
        for a, pair in zip(a_parts, a_pairs):
            if pair:
                kw = a[0].shape[1]
                in_specs += [pl.BlockSpec((tm, kw), lambda i, s: (jnp.minimum(tile(i), p_tiles - 1), 0)),
                             pl.BlockSpec((tm, kw), lambda i, s: (jnp.maximum(tile(i) - p_tiles, 0), 0))]
            else:
                in_specs.append(pl.BlockSpec((tm, a.shape[1]), lambda i, s: (tile(i), 0)))
        in_specs.append(pl.BlockSpec((None, kdim, tn), lambda i, s: (layer, 0, s)))
    else:
        (a,) = a_parts
        assert not a_pairs[0]
        tn, ka = n, a.shape[1]
        tk = max(t for t in range(MXU_DEPTH, ka // nch + 1, MXU_DEPTH) if ka % t == 0)
        n_steps = ka // tk
        if ka > kdim:
            last_rows = kdim - (n_steps - 1) * tk
            assert 0 < last_rows < tk and n_steps > nch
        in_specs = [pl.BlockSpec((tm, tk), lambda i, s: (tile(i), s)),
                    pl.BlockSpec((None, tk, n), lambda i, s: (layer, s, 0))]
    in_specs += (group_specs if len(xs) == 2 else [chunk_spec(n)]) + [const((1, n))]
    args = [v for a, pair in zip(a_parts, a_pairs) for v in (a if pair else [a])] + [b] + xs + [g]
    if second == "norm":
        in_specs.append(const((1, n)))
        args.append(g2)
    if ple:
        assert split == "n"
        in_specs += [pl.BlockSpec((tm, pe.shape[1]), lambda i, s: (tile(i), 0)),
                     pl.BlockSpec((None, wple.shape[1], tn), lambda i, s: (layer, 0, s))]
        args += [pe, wple]
    if gates:
        in_specs += [const(wif.shape), const(bif.shape)]
        args += [wif, bif]
    if split_out:
        out_specs = list(group_specs)
        out_shape = [jax.ShapeDtypeStruct((split_out, n), F32), jax.ShapeDtypeStruct((m - split_out, n), F32)]
    else:
        out_specs, out_shape = [chunk_spec(n)], [jax.ShapeDtypeStruct((m, n), F32)]
    if second:
        out_specs.append(chunk_spec(n))
        out_shape.append(jax.ShapeDtypeStruct((m, n), BF16))
    if gates:
        out_specs.append(chunk_spec(GATE_PAD))
        out_shape.append(jax.ShapeDtypeStruct((m, GATE_PAD), F32))
    return pl.pallas_call(
        functools.partial(_mm_res_kernel, a_pairs=tuple(a_pairs), split=split, n_tiles=n_tiles, n_steps=n_steps,
                          nch=nch, tn=tn, rb=rb, ple=ple, second=second, gates=gates, n_heads=n_heads,
                          n_x=len(xs), n_xo=2 if split_out else 1, p_chunks=p_chunks, p_tiles=p_tiles,
                          last_rows=last_rows),
        grid=(n_tiles + 1, n_steps),
        in_specs=in_specs, out_specs=out_specs, out_shape=out_shape,
        scratch_shapes=[pltpu.VMEM((tm, n), F32), pltpu.VMEM((tm, n), F32)],
        compiler_params=_params("arbitrary", "arbitrary"), name=name,
    )(*args)


def _split3(x):
    hi = x.astype(BF16)
    r1 = x - hi.astype(F32)
    mid = r1.astype(BF16)
    lo = (r1 - mid.astype(F32)).astype(BF16)
    return hi, mid, lo


def _mlstm_kernel(zq_ref, zk_ref, zv_ref, zo_ref, gate_ref, c0_ref, n0_ref, m0_ref,
                  gmh_ref, y_ref, c_ref, n_ref, m_ref, *, n_heads, dh, t):
    @pl.when(pl.program_id(1) == 0)
    def _():
        c_ref[...] = c0_ref[...]
        n_ref[...] = n0_ref[...]
        m_ref[...] = m0_ref[...]

    row = lax.broadcasted_iota(jnp.int32, (t, t), 0)
    col = lax.broadcasted_iota(jnp.int32, (t, t), 1)
    causal = row >= col
    tri = causal.astype(BF16)
    tri_t = (col >= row).astype(BF16)
    g_col = gate_ref[...]
    g_row = g_col.T
    cum_col = sum(_dot(tri, p) for p in _split3(g_col))
    cum_row = sum(_dot(p, tri_t) for p in _split3(g_row))
    scale = dh ** -0.5

    for h in range(n_heads):
        sl = slice(h * dh, (h + 1) * dh)
        q = zq_ref[:, sl]
        k = zk_ref[:, sl]
        v = zv_ref[:, sl]
        i_col = g_col[:, h:h + 1]
        i_row = g_row[h:h + 1, :]
        b_col = cum_col[:, n_heads + h:n_heads + h + 1]
        b_row = cum_row[n_heads + h:n_heads + h + 1, :]
        m_prev = m_ref[0, :, h:h + 1]
        c_prev = c_ref[0, h]
        n_prev = n_ref[0, h:h + 1, :]

        d = jnp.where(causal, b_col - b_row + i_row, -jnp.inf)
        inter = b_col + m_prev
        mt = jnp.maximum(inter, jnp.max(d, axis=1, keepdims=True))
        s = lax.dot_general(q, k, (((1,), (1,)), ((), ())), preferred_element_type=F32)
        s = s * scale * jnp.exp(d - mt)
        sc_t = jnp.exp(inter - mt)
        num = sc_t * _dot(q, c_prev.astype(BF16)) + _dot(s.astype(BF16), v)
        qn = jnp.sum(q.astype(F32) * n_prev, axis=1, keepdims=True)
        den = sc_t * qn + jnp.sum(s, axis=1, keepdims=True)
        hh = num / jnp.maximum(jnp.abs(den), jnp.exp(-mt))

        b_last = b_col[t - 1:t, :]
        m_new = mt[t - 1:t, :]
        wl = jnp.exp(b_last - b_col + i_col - m_new)
        sc = jnp.exp(b_last + m_prev - m_new)
        kw = k.astype(F32) * wl
        kv = lax.dot_general(kw.astype(BF16), v, (((0,), (0,)), ((), ())), preferred_element_type=F32)
        c_ref[0, h] = sc * c_prev + kv * scale
        n_ref[0, h:h + 1, :] = sc * n_prev + jnp.sum(kw, axis=0, keepdims=True) * scale
        m_ref[0, :, h:h + 1] = m_new

        hn = _rms(hh, gmh_ref[:, sl])
        y_ref[:, sl] = (hn * jax.nn.sigmoid(zo_ref[:, sl].astype(F32))).astype(BF16)


def mlstm(z, gates, c0, n0, m0, gmh, layer, row_off, bsz, seq, n_heads, dh):
    d_a = n_heads * dh
    t = min(256, seq)
    nc = seq // t
    r0 = row_off // t
    rmap = lambda b, c: r0 + b * nc + c
    zspec = lambda j: pl.BlockSpec((t, d_a), lambda b, c, j=j: (rmap(b, c), j))
    return pl.pallas_call(
        functools.partial(_mlstm_kernel, n_heads=n_heads, dh=dh, t=t),
        grid=(bsz, nc),
        in_specs=[zspec(0), zspec(1), zspec(2), zspec(3),
                  pl.BlockSpec((t, GATE_PAD), lambda b, c: (rmap(b, c), 0)),
                  pl.BlockSpec((None, 1, n_heads, dh, dh), lambda b, c: (layer, b, 0, 0, 0)),
                  pl.BlockSpec((1, n_heads, dh), lambda b, c: (b, 0, 0)),
                  pl.BlockSpec((1, 1, LANES), lambda b, c: (b, 0, 0)),
                  pl.BlockSpec((1, d_a), lambda b, c: (0, 0))],
        out_specs=[pl.BlockSpec((t, d_a), lambda b, c: (b * nc + c, 0)),
                   pl.BlockSpec((1, n_heads, dh, dh), lambda b, c: (b, 0, 0, 0)),
                   pl.BlockSpec((1, n_heads, dh), lambda b, c: (b, 0, 0)),
                   pl.BlockSpec((1, 1, LANES), lambda b, c: (b, 0, 0))],
        out_shape=[jax.ShapeDtypeStruct((bsz * seq, d_a), BF16),
                   jax.ShapeDtypeStruct((bsz, n_heads, dh, dh), F32),
                   jax.ShapeDtypeStruct((bsz, n_heads, dh), F32),
                   jax.ShapeDtypeStruct((bsz, 1, LANES), F32)],
        compiler_params=_params("parallel", "arbitrary"), name="mlstm",
    )(z, z, z, z, gates, c0, n0, m0, gmh)


S5_ROWS = 512


def _s5_disc_kernel(lr_ref, li_ref, ldt_ref, br_ref, bi_ref, abr_ref, abi_ref, bbr_ref, bbi_ref):
    dt = jnp.exp(ldt_ref[...])
    lr = lr_ref[...]
    li = li_ref[...]
    mag = jnp.exp(lr * dt)
    ab_re = mag * jnp.cos(li * dt)
    ab_im = mag * jnp.sin(li * dt)
    den = lr * lr + li * li
    nr = ab_re - 1.0
    coef_re = (nr * lr + ab_im * li) / den
    coef_im = (ab_im * lr - nr * li) / den
    br = br_ref[...]
    bi = bi_ref[...]
    abr_ref[...] = ab_re
    abi_ref[...] = ab_im
    bbr_ref[...] = coef_re * br - coef_im * bi
    bbi_ref[...] = coef_re * bi + coef_im * br


def s5_discretise(lam_re, lam_im, log_dt, b_re, b_im):
    g, p, j = b_re.shape
    rep = lambda a: jnp.repeat(a, j, axis=0)
    br_t = jnp.swapaxes(b_re, 1, 2).reshape(g * j, p)
    bi_t = jnp.swapaxes(b_im, 1, 2).reshape(g * j, p)
    shp = jax.ShapeDtypeStruct((g * j, p), F32)
    abr, abi, bbr, bbi = pl.pallas_call(
        _s5_disc_kernel, out_shape=[shp, shp, shp, shp], name="s5_disc",
    )(rep(lam_re), rep(lam_im), rep(log_dt[:, None]), br_t, bi_t)
    first = lambda a: a.reshape(g, j, p)[:, 0]
    return first(abr), first(abi), bbr.reshape(g, j, p), bbi.reshape(g, j, p)


def _s5_scan_kernel(*refs, s_rows, tc, nq, gw, slab, emit_y):
    if emit_y:
        u_ref, h0_ref, a_ref, bblk_ref, cblk_ref, d_ref, wglu_ref, y_ref, hout_ref, bu_scr = refs
    else:
        u_ref, h0_ref, a_ref, bblk_ref, hout_ref, bu_scr = refs

    @pl.when(pl.program_id(0) == 0)
    def _():
        hout_ref[...] = h0_ref[...]

    u = u_ref[...]
    kq = u.shape[1] // nq
    for q in range(nq):
        bu_scr[:, q * 2 * gw:(q + 1) * 2 * gw] = _dot(u[:, q * kq:(q + 1) * kq], bblk_ref[q])

    for q in range(nq):
        for lo in range(0, gw, slab):
            re = slice(q * 2 * gw + lo, q * 2 * gw + lo + slab)
            im = slice(q * 2 * gw + gw + lo, q * 2 * gw + gw + lo + slab)
            ar = a_ref[:, re]
            ai = a_ref[:, im]

            def body(step, carry, re=re, im=im, ar=ar, ai=ai):
                hr, hi = carry
                rows = pl.ds(pl.multiple_of(step * s_rows, s_rows), s_rows)
                nr = ar * hr - ai * hi + bu_scr[rows, re]
                ni = ar * hi + ai * hr + bu_scr[rows, im]
                if emit_y:
                    bu_scr[rows, re] = nr
                    bu_scr[rows, im] = ni
                return nr, ni

            hr, hi = lax.fori_loop(0, tc, body, (hout_ref[:, re], hout_ref[:, im]), unroll=2)
            hout_ref[:, re] = hr
            hout_ref[:, im] = hi

    if emit_y:
        ys = [_dot(bu_scr[:, q * 2 * gw:(q + 1) * 2 * gw].astype(BF16), cblk_ref[q]) for q in range(nq)]
        y = jnp.concatenate(ys, axis=1) + d_ref[...] * u.astype(F32)
        zg = 0.5 * y * (1.0 + jnp.tanh(math.sqrt(2.0 / math.pi) * (y + 0.044715 * (y * y * y))))
        y_ref[...] = (zg * jax.nn.sigmoid(_dot(zg.astype(BF16), wglu_ref[...]))).astype(BF16)


def s5_scan(u_perm, h0, a_blk, bblk, cblk, d, wglu, s_rows, emit_y):
    rows, d_b = u_perm.shape
    nq, kq, w2 = bblk.shape
    gw = w2 // 2
    steps = rows // s_rows
    tc = _pick(steps, max(1, S5_ROWS // s_rows))
    rb = tc * s_rows
    wtot = nq * w2
    const2 = lambda c: (0, 0)
    in_specs = [pl.BlockSpec((rb, d_b), lambda c: (c, 0)),
                pl.BlockSpec((s_rows, wtot), const2),
                pl.BlockSpec((1, wtot), const2),
                pl.BlockSpec((nq, kq, w2), lambda c: (0, 0, 0))]
    args = [u_perm, h0, a_blk, bblk]
    out_specs = [pl.BlockSpec((s_rows, wtot), const2)]
    out_shape = [jax.ShapeDtypeStruct((s_rows, wtot), F32)]
    if emit_y:
        in_specs += [pl.BlockSpec((nq, w2, kq), lambda c: (0, 0, 0)),
                     pl.BlockSpec((1, d_b), const2),
                     pl.BlockSpec((d_b, d_b), const2)]
        args += [cblk, d, wglu]
        out_specs = [pl.BlockSpec((rb, d_b), lambda c: (c, 0))] + out_specs
        out_shape = [jax.ShapeDtypeStruct((rows, d_b), BF16)] + out_shape
    return pl.pallas_call(
        functools.partial(_s5_scan_kernel, s_rows=s_rows, tc=tc, nq=nq, gw=gw,
                          slab=min(512, gw), emit_y=emit_y),
        grid=(steps // tc,),
        in_specs=in_specs, out_specs=out_specs, out_shape=out_shape,
        scratch_shapes=[pltpu.VMEM((rb, wtot), F32)],
        compiler_params=_params("arbitrary"), name="s5_scan" if emit_y else "s5_ends",
    )(*args)


def _s5_chain_kernel(sfin_ref, a_ref, hin_ref, *, seg_len, n_seg, nq, gw):
    for q in range(nq):
        re = slice(q * 2 * gw, q * 2 * gw + gw)
        im = slice(q * 2 * gw + gw, (q + 1) * 2 * gw)
        br, bi = a_ref[:, re], a_ref[:, im]
        pr, pi = jnp.ones_like(br), jnp.zeros_like(br)
        e = seg_len
        while e:
            if e & 1:
                pr, pi = pr * br - pi * bi, pr * bi + pi * br
            br, bi = br * br - bi * bi, 2.0 * br * bi
            e >>= 1
        hr, hi = jnp.zeros_like(pr), jnp.zeros_like(pr)
        for s in range(n_seg):
            hin_ref[s:s + 1, re] = hr
            hin_ref[s:s + 1, im] = hi
            hr, hi = (pr * hr - pi * hi + sfin_ref[s:s + 1, re],
                      pr * hi + pi * hr + sfin_ref[s:s + 1, im])


def s5_chain(sfin, a_blk, seg_len, nq):
    n_seg, wtot = sfin.shape
    return pl.pallas_call(
        functools.partial(_s5_chain_kernel, seg_len=seg_len, n_seg=n_seg, nq=nq, gw=wtot // nq // 2),
        out_shape=jax.ShapeDtypeStruct((n_seg, wtot), F32), name="s5_chain",
    )(sfin, a_blk)


def _conv_kernel(val_ref, gate_ref, cache_ref, w_ref, b_ref, lng_ref, lnb_ref, y_ref, cout_ref, xp_scr, xs_scr,
                 *, t, width):
    c = pl.program_id(1)

    @pl.when(c == 0)
    def _():
        xp_scr[0:CACHE_PAD, :] = cache_ref[0]

    @pl.when(c > 0)
    def _():
        xp_scr[0:CACHE_PAD, :] = xp_scr[t:t + CACHE_PAD, :]

    xp_scr[CACHE_PAD:CACHE_PAD + t, :] = val_ref[...].astype(F32) * jax.nn.sigmoid(gate_ref[...].astype(F32))
    first = CACHE_PAD - (width - 1)
    acc = None
    for r in range(min(SUBLANES, width)):
        taps = range(r, width, SUBLANES)
        rows = t + SUBLANES * (len(taps) - 1)
        xs_scr[0:rows, :] = xp_scr[first + r:first + r + rows, :]
        for a, j in enumerate(taps):
            term = w_ref[j:j + 1, :] * xs_scr[SUBLANES * a:SUBLANES * a + t, :]
            acc = term if acc is None else acc + term
    y = acc + b_ref[...]
    mu = jnp.mean(y, axis=-1, keepdims=True)
    yc = y - mu
    var = jnp.mean(yc * yc, axis=-1, keepdims=True)
    y = yc * lax.rsqrt(var + EPS) * lng_ref[...] + lnb_ref[...]
    y_ref[...] = (y * jax.nn.sigmoid(y)).astype(BF16)
    cout_ref[0] = xp_scr[t:t + CACHE_PAD, :]


def conv_module(z, cache, w, b, lng, lnb, row_off, bsz, seq, col_val, col_gate):
    width, d_c = w.shape
    t = min(256, seq)
    assert t >= CACHE_PAD
    nc = seq // t
    r0 = row_off // t
    rmap = lambda bb, c: r0 + bb * nc + c
    vec = pl.BlockSpec((1, d_c), lambda bb, c: (0, 0))
    return pl.pallas_call(
        functools.partial(_conv_kernel, t=t, width=width),
        grid=(bsz, nc),
        in_specs=[pl.BlockSpec((t, d_c), lambda bb, c: (rmap(bb, c), col_val)),
                  pl.BlockSpec((t, d_c), lambda bb, c: (rmap(bb, c), col_gate)),
                  pl.BlockSpec((1, CACHE_PAD, d_c), lambda bb, c: (bb, 0, 0)),
                  pl.BlockSpec((width, d_c), lambda bb, c: (0, 0)),
                  vec, vec, vec],
        out_specs=[pl.BlockSpec((t, d_c), lambda bb, c: (bb * nc + c, 0)),
                   pl.BlockSpec((1, CACHE_PAD, d_c), lambda bb, c: (bb, 0, 0))],
        out_shape=[jax.ShapeDtypeStruct((bsz * seq, d_c), BF16),
                   jax.ShapeDtypeStruct((bsz, CACHE_PAD, d_c), F32)],
        scratch_shapes=[pltpu.VMEM((CACHE_PAD + t, d_c), F32), pltpu.VMEM((CACHE_PAD + t, d_c), F32)],
        compiler_params=_params("parallel", "arbitrary"), name="conv",
    )(z, z, cache, w, b, lng, lnb)


def _block_diag(w, gq):
    g, r, c = w.shape
    eye = jnp.eye(gq, dtype=w.dtype)
    w = w.reshape(g // gq, gq, r, c)
    return jnp.einsum("qgrc,gh->qgrhc", w, eye).reshape(g // gq, gq * r, gq * c)


def _state_to_lanes(re, im, gq):
    s, g, p = re.shape
    both = jnp.stack([re.reshape(s, g // gq, gq * p), im.reshape(s, g // gq, gq * p)], axis=2)
    return both.reshape(s, -1)


def _lanes_to_state(h, g, p, gq):
    s = h.shape[0]
    both = h.reshape(s, g // gq, 2, gq * p)
    return both[:, :, 0].reshape(s, g, p), both[:, :, 1].reshape(s, g, p)


def _to_step_major(x, n_seq):
    rows, d = x.shape
    return x.reshape(n_seq, rows // n_seq, d).swapaxes(0, 1).reshape(rows, d)


def _from_step_major(x, n_seq):
    rows, d = x.shape
    return x.reshape(rows // n_seq, n_seq, d).swapaxes(0, 1).reshape(rows, d)


PROMPT_SEGMENTS = 32


def kernel(x_prompt, x_sample, p_prompt, p_sample, state_mlstm_c, state_mlstm_n, state_mlstm_m, state_s5_re, state_s5_im, cache_conv, g_pre_mix, w_in, b_igate, b_fgate, g_mh, s5_lam_re, s5_lam_im, s5_log_dt, s5_b_re, s5_b_im, s5_c_re, s5_c_im, s5_d, s5_w_glu, conv_w, conv_b, conv_ln_g, conv_ln_b, w_out, g_post_mix, g_pre_ffn, w_ffn_gate, w_ffn_up, w_ffn_down, g_post_ffn, w_ple, w_ple_gate, g_post_ple):
    depth = w_in.shape[0]
    bp, lp, d_model = x_prompt.shape
    bs, ls, _ = x_sample.shape
    n_heads = b_igate.shape[1]
    d_a = g_mh.shape[1]
    dh = d_a // n_heads
    g_b, p_b, j_b = s5_b_re.shape[1:]
    d_b = g_b * j_b
    width, d_c = conv_w.shape[1:]
    d_ff = w_ffn_gate.shape[2]
    assert bp == 1 and 2 * n_heads <= GATE_PAD and d_b == d_c and d_a % d_b == 0
    gq = 256 // j_b
    nq = g_b // gq
    np_rows, ns_rows = bp * lp, bs * ls
    n_seg = min(PROMPT_SEGMENTS, lp // 8)
    seg_len = lp // n_seg
    col_u = 4 * d_a // d_b
    col_val, col_gate = col_u + 1, col_u + 2
    d_ff_pad = -(-d_ff // 1024) * 1024

    x = (x_prompt.reshape(np_rows, d_model), x_sample.reshape(ns_rows, d_model))
    row = lambda v: v.reshape(1, -1).astype(F32)
    zeros = lambda *s: jnp.zeros(s, F32)
    pad_m = lambda m: jnp.pad(m, ((0, 0), (0, LANES - n_heads)))[:, None, :]
    pad_cache = lambda c: jnp.pad(c, ((0, 0), (CACHE_PAD - (width - 1), 0), (0, 0)))

    n_qkvo = 4 * d_a
    w_ifs = [jnp.pad(w_in[i][:, n_qkvo:n_qkvo + 2 * n_heads], ((0, 0), (0, GATE_PAD - 2 * n_heads))).astype(BF16)
             for i in range(depth)]
    b_ifs = [jnp.pad(jnp.concatenate([b_igate[i], b_fgate[i]]), (0, GATE_PAD - 2 * n_heads)).reshape(1, GATE_PAD)
             for i in range(depth)]
    w_gate_all, w_up_all = w_ffn_gate.astype(BF16), w_ffn_up.astype(BF16)
    w_down_all = w_ffn_down.astype(BF16)
    w_out_all, w_pgate_all, w_ple_all = w_out.astype(BF16), w_ple_gate.astype(BF16), w_ple.astype(BF16)
    zero_c = zeros(1, bp, n_heads, dh, dh)
    outs = {k: [] for k in ("pc", "sc", "pn", "pm", "pre", "pim", "pcv", "sn", "sm", "sre", "sim", "scv")}
    for i in range(depth):
        w_main = jnp.concatenate([w_in[i][:, :n_qkvo], w_in[i][:, n_qkvo + 2 * n_heads:]], axis=1).astype(BF16)

        ab_re, ab_im, bb_re, bb_im = s5_discretise(s5_lam_re[i], s5_lam_im[i], s5_log_dt[i], s5_b_re[i], s5_b_im[i])
        bblk = jnp.concatenate([_block_diag(bb_re, gq), _block_diag(bb_im, gq)], axis=2).astype(BF16)
        c_re_t = jnp.swapaxes(s5_c_re[i], 1, 2)
        c_im_t = jnp.swapaxes(s5_c_im[i], 1, 2)
        cblk = jnp.concatenate([_block_diag(c_re_t, gq), _block_diag(-c_im_t, gq)], axis=1).astype(BF16)
        a_blk = _state_to_lanes(ab_re[None], ab_im[None], gq)
        wglu = s5_w_glu[i].astype(BF16)
        d_row = row(s5_d[i])

        if i == 0:
            h, gates = rms_gates(x[0], x[1], row(g_pre_mix[i]), w_ifs[i], b_ifs[i], n_heads)
        z = mm(h, w_main, BF16)

        ya_p, c1, n1, m1 = mlstm(z, gates, zero_c, zeros(bp, n_heads, dh), zeros(bp, 1, LANES), row(g_mh[i]),
                                 0, 0, bp, lp, n_heads, dh)
        ya_s, c2, n2, m2 = mlstm(z, gates, state_mlstm_c, state_mlstm_n[i], pad_m(state_mlstm_m[i]), row(g_mh[i]),
                                 i, np_rows, bs, ls, n_heads, dh)

        u_all = z[:, 4 * d_a:4 * d_a + d_b]
        u_p = _to_step_major(u_all[:np_rows], n_seg)
        u_s = _to_step_major(u_all[np_rows:], bs)
        wtot = a_blk.shape[1]
        (sfin,) = s5_scan(u_p, zeros(n_seg, wtot), a_blk, bblk, None, None, None, n_seg, False)
        hin = s5_chain(sfin, a_blk, seg_len, nq)
        yb_p, hfin_p = s5_scan(u_p, hin, a_blk, bblk, cblk, d_row, wglu, n_seg, True)
        yb_s, hfin_s = s5_scan(u_s, _state_to_lanes(state_s5_re[i], state_s5_im[i], gq), a_blk, bblk, cblk,
                               d_row, wglu, bs, True)
        yb_p, yb_s = _from_step_major(yb_p, n_seg), _from_step_major(yb_s, bs)
        re_p, im_p = _lanes_to_state(hfin_p[n_seg - 1:], g_b, p_b, gq)
        re_s, im_s = _lanes_to_state(hfin_s, g_b, p_b, gq)

        cw, cb, lg, lb = conv_w[i], row(conv_b[i]), row(conv_ln_g[i]), row(conv_ln_b[i])
        yc_p, cv_p = conv_module(z, zeros(bp, CACHE_PAD, d_c), cw, cb, lg, lb, 0, bp, lp, col_val, col_gate)
        yc_s, cv_s = conv_module(z, pad_cache(cache_conv[i]), cw, cb, lg, lb, np_rows, bs, ls, col_val, col_gate)

        x, h = mm_res([(ya_p, ya_s), (yb_p, yb_s), (yc_p, yc_s)], w_out_all, x, row(g_post_mix[i]), split="n",
                      layer=i, second="norm", g2=row(g_pre_ffn[i]), name="mm_out")

        f = mm_swiglu(h, w_gate_all, w_up_all, i, d_ff_pad)
        x, xb = mm_res([f], w_down_all, x, row(g_post_ffn[i]), split="k", layer=i, second="cast",
                       name="mm_down")

        pe = jnp.concatenate([p_prompt[i].reshape(np_rows, -1), p_sample[i].reshape(ns_rows, -1)], axis=0).astype(BF16)
        ple_args = dict(split="n", layer=i, pe=pe, wple=w_ple_all, name="mm_ple")
        if i + 1 < depth:
            x, h, gates = mm_res([xb], w_pgate_all, x, row(g_post_ple[i]), second="norm",
                                 g2=row(g_pre_mix[i + 1]), wif=w_ifs[i + 1], bif=b_ifs[i + 1], n_heads=n_heads,
                                 **ple_args)
        else:
            x = mm_res([xb], w_pgate_all, x, row(g_post_ple[i]), split_out=np_rows, **ple_args)

        first = CACHE_PAD - (width - 1)
        outs["pc"].append(c1); outs["pn"].append(n1); outs["pm"].append(m1[:, 0, :n_heads])
        outs["pre"].append(re_p); outs["pim"].append(im_p); outs["pcv"].append(cv_p[:, first:])
        outs["sc"].append(c2); outs["sn"].append(n2); outs["sm"].append(m2[:, 0, :n_heads])
        outs["sre"].append(re_s); outs["sim"].append(im_s); outs["scv"].append(cv_s[:, first:])

    st = lambda k: jnp.stack(outs[k])
    return (x[0].reshape(bp, lp, d_model), x[1].reshape(bs, ls, d_model),
            st("pc"), st("pn"), st("pm"), st("pre"), st("pim"), st("pcv"),
            st("sc"), st("sn"), st("sm"), st("sre"), st("sim"), st("scv"))
```

```python
import functools
import math

import jax
import jax.numpy as jnp
from jax import lax
from jax.experimental import pallas as pl
from jax.experimental.pallas import tpu as pltpu

F32 = jnp.float32
BF16 = jnp.bfloat16
EPS = 1e-6
LANES = 128
SUBLANES = 8
VMEM_LIMIT = 56 * 1024 * 1024
VMEM_LIMIT_WIDE = 60 * 1024 * 1024
GATE_PAD = LANES
CACHE_PAD = 32


def _pick(n, pref):
    t = min(pref, n)
    while n % t:
        t //= 2
    return t


def _params(*sem):
    return pltpu.CompilerParams(dimension_semantics=sem, vmem_limit_bytes=VMEM_LIMIT)


def _rms(x, g):
    return x * lax.rsqrt(jnp.mean(x * x, axis=-1, keepdims=True) + EPS) * g


def _log_sigmoid(x):
    return jnp.minimum(x, 0.0) - jnp.log1p(jnp.exp(-jnp.abs(x)))


def _dot(a, b):
    return jnp.dot(a, b, preferred_element_type=F32)


def _gate_act(gt, n_heads):
    lane = lax.broadcasted_iota(jnp.int32, gt.shape, 1)
    return jnp.where(lane >= n_heads, _log_sigmoid(gt), gt)


def _rms_gates_kernel(x0_ref, x1_ref, g_ref, wif_ref, bias_ref, h_ref, gate_ref, *, n_heads, tiles0):
    x = jnp.where(pl.program_id(0) < tiles0, x0_ref[...], x1_ref[...])
    hb = _rms(x, g_ref[...]).astype(BF16)
    h_ref[...] = hb
    gate_ref[...] = _gate_act(_dot(hb, wif_ref[...]) + bias_ref[...], n_heads)


def rms_gates(x0, x1, g, wif, bias, n_heads):
    (n0, d), n1 = x0.shape, x1.shape[0]
    tm = _pick(math.gcd(n0, n1), 256)
    tiles0 = n0 // tm
    return pl.pallas_call(
        functools.partial(_rms_gates_kernel, n_heads=n_heads, tiles0=tiles0),
        grid=((n0 + n1) // tm,),
        in_specs=[pl.BlockSpec((tm, d), lambda i: (jnp.minimum(i, tiles0 - 1), 0)),
                  pl.BlockSpec((tm, d), lambda i: (jnp.maximum(i - tiles0, 0), 0)),
                  pl.BlockSpec((1, d), lambda i: (0, 0)),
                  pl.BlockSpec((d, GATE_PAD), lambda i: (0, 0)),
                  pl.BlockSpec((1, GATE_PAD), lambda i: (0, 0))],
        out_specs=[pl.BlockSpec((tm, d), lambda i: (i, 0)),
                   pl.BlockSpec((tm, GATE_PAD), lambda i: (i, 0))],
        out_shape=[jax.ShapeDtypeStruct((n0 + n1, d), BF16),
                   jax.ShapeDtypeStruct((n0 + n1, GATE_PAD), F32)],
        compiler_params=_params("parallel"), name="rms_gates",
    )(x0, x1, g, wif, bias)


def _mm_kernel(a_ref, b_ref, o_ref):
    o_ref[...] = _dot(a_ref[...], b_ref[...]).astype(o_ref.dtype)


def mm(a, b, out_dtype):
    m, k = a.shape
    n = b.shape[1]
    tm, tn = _pick(m, 1024), _pick(n, 1024)
    return pl.pallas_call(
        _mm_kernel,
        grid=(m // tm, n // tn),
        in_specs=[pl.BlockSpec((tm, k), lambda i, j: (i, 0)),
                  pl.BlockSpec((k, tn), lambda i, j: (0, j))],
        out_specs=pl.BlockSpec((tm, tn), lambda i, j: (i, j)),
        out_shape=jax.ShapeDtypeStruct((m, n), out_dtype),
        compiler_params=_params("parallel", "arbitrary"), name="mm_in",
    )(a, b)


def _mm_swiglu_kernel(a_ref, bg_ref, bu_ref, o_ref, *, d_ff):
    j, tn = pl.program_id(1), o_ref.shape[1]
    ragged = (j + 1) * tn > d_ff

    def swiglu():
        a = a_ref[...]
        gate = _dot(a, bg_ref[...])
        return gate * jax.nn.sigmoid(gate) * _dot(a, bu_ref[...])

    @pl.when(jnp.logical_not(ragged))
    def _():
        o_ref[...] = swiglu().astype(o_ref.dtype)

    @pl.when(ragged)
    def _():
        f = swiglu()
        col = j * tn + lax.broadcasted_iota(jnp.int32, f.shape, 1)
        o_ref[...] = jnp.where(col < d_ff, f, 0.0).astype(o_ref.dtype)


def mm_swiglu(a, bg, bu, layer, n):
    m, k = a.shape
    d_ff = bg.shape[2]
    tm, tn = _pick(m, 1024), _pick(n, 512)
    wspec = pl.BlockSpec((None, k, tn), lambda i, j: (layer, 0, jnp.minimum(j, (d_ff - 1) // tn)))
    return pl.pallas_call(
        functools.partial(_mm_swiglu_kernel, d_ff=d_ff),
        grid=(m // tm, n // tn),
        in_specs=[pl.BlockSpec((tm, k), lambda i, j: (i, 0)), wspec, wspec],
        out_specs=pl.BlockSpec((tm, tn), lambda i, j: (i, j)),
        out_shape=jax.ShapeDtypeStruct((m, n), BF16),
        compiler_params=_params("parallel", "arbitrary"), name="mm_swiglu",
    )(a, bg, bu)


RES_ROW_TILE = 512
EPI_CHUNKS = 8
MXU_DEPTH = 256


def _mm_res_kernel(*refs, a_pairs, split, n_tiles, n_steps, nch, tn, rb, ple, second, gates, n_heads, n_x, n_xo,
                   p_chunks, p_tiles, last_rows):
    refs = list(refs)
    a_refs = [(refs.pop(0), refs.pop(0)) if pair else refs.pop(0) for pair in a_pairs]
    b_ref = refs.pop(0)
    x_refs = [refs.pop(0) for _ in range(n_x)]
    g_ref = refs.pop(0)
    g2_ref = refs.pop(0) if second == "norm" else None
    pe_ref, wple_ref = (refs.pop(0), refs.pop(0)) if ple else (None, None)
    wif_ref, bif_ref = (refs.pop(0), refs.pop(0)) if gates else (None, None)
    xo_refs = [refs.pop(0) for _ in range(n_xo)]
    h_ref = refs.pop(0) if second else None
    gate_ref = refs.pop(0) if gates else None
    acc_refs = (refs.pop(0), refs.pop(0))
    i, s = pl.program_id(0), pl.program_id(1)
    in_first = jnp.where(i == 0, 0, (i - 1) * nch + jnp.minimum(s, nch - 1)) < p_chunks

    def load_a(a_ref):
        if isinstance(a_ref, tuple):
            return jnp.where(i < p_tiles, a_ref[0][...], a_ref[1][...])
        return a_ref[...]

    def matmul(acc_ref, ragged=False):
        if split == "n":
            part, k0 = None, 0
            for a_ref in a_refs:
                a = load_a(a_ref)
                kw = a.shape[1]
                d = _dot(a, b_ref[k0:k0 + kw, :])
                part = d if part is None else part + d
                k0 += kw
            if ple:
                part = _dot(pe_ref[...], wple_ref[...]) * jax.nn.sigmoid(part)
            acc_ref[:, pl.ds(pl.multiple_of(s * tn, tn), tn)] = part
        else:
            b = b_ref[...]
            if ragged:
                b = jnp.where(lax.broadcasted_iota(jnp.int32, b.shape, 0) < last_rows, b, jnp.zeros_like(b))
            acc_ref[...] += _dot(load_a(a_refs[0]), b)

    def epilogue(acc_ref):
        rows = pl.ds(pl.multiple_of(s * rb, rb), rb)
        f = acc_ref[rows, :]
        if split == "k":
            acc_ref[rows, :] = jnp.zeros_like(f)
        x = x_refs[0][...] if n_x == 1 else jnp.where(in_first, x_refs[0][...], x_refs[1][...])
        xn = x + _rms(f, g_ref[...])
        if n_xo == 1:
            xo_refs[0][...] = xn
        else:
            xo_refs[0][...] = jnp.where(in_first, xn, xo_refs[0][...])
            xo_refs[1][...] = jnp.where(in_first, xo_refs[1][...], xn)
        if second == "norm":
            hb = _rms(xn, g2_ref[...]).astype(BF16)
            h_ref[...] = hb
            if gates:
                gate_ref[...] = _gate_act(_dot(hb, wif_ref[...]) + bif_ref[...], n_heads)
        elif second == "cast":
            h_ref[...] = xn.astype(BF16)

    @pl.when((i == 0) & (s == 0))
    def _():
        for ref in list(acc_refs) + (xo_refs if n_xo == 2 else []):
            ref[...] = jnp.zeros_like(ref)

    for parity in range(2):
        mine = (i < n_tiles) & (lax.rem(i, 2) == parity)

        @pl.when(mine & (s < nch))
        def _(parity=parity):
            epilogue(acc_refs[1 - parity])
            matmul(acc_refs[parity])

        if n_steps > nch:
            whole = n_steps if last_rows is None else n_steps - 1

            @pl.when(mine & (s >= nch) & (s < whole))
            def _(parity=parity):
                matmul(acc_refs[parity])

            if whole < n_steps:
                @pl.when(mine & (s == whole))
                def _(parity=parity):
                    matmul(acc_refs[parity], ragged=True)

    @pl.when((i == n_tiles) & (s < nch))
    def _():
        epilogue(acc_refs[1 - n_tiles % 2])


def mm_res(a_parts, b, x, g, *, split, second=None, g2=None, pe=None, wple=None, wif=None, bif=None,
           n_heads=0, split_out=None, layer=0, chunks=EPI_CHUNKS, name="mm_res"):
    xs = list(x) if isinstance(x, (tuple, list)) else [x]
    m = sum(v.shape[0] for v in xs)
    kdim, n = b.shape[1:]
    last_rows = None
    tm = _pick(m, RES_ROW_TILE)
    n_tiles, nch = m // tm, chunks
    rb = tm // nch
    ple, gates = pe is not None, wif is not None
    once = pl.Buffered(1)
    const = lambda shape: pl.BlockSpec(shape, lambda i, s: (0, 0), pipeline_mode=once)
    tile = lambda i: jnp.minimum(i, n_tiles - 1)
    chunk = lambda i, s: jnp.where(i == 0, 0, (i - 1) * nch + jnp.minimum(s, nch - 1))
    chunk_spec = lambda w: pl.BlockSpec((rb, w), lambda i, s: (chunk(i, s), 0))
    a_pairs = [isinstance(a, (tuple, list)) for a in a_parts]
    first_rows = {v[0].shape[0] for v in [xs] + [a for a, pair in zip(a_parts, a_pairs) if pair] if len(v) == 2}
    first_rows |= {split_out} if split_out else set()
    assert len(first_rows) <= 1
    first_rows = first_rows.pop() if first_rows else 0
    assert first_rows % tm == 0
    p_chunks, p_tiles = first_rows // rb, first_rows // tm
    group_specs = [pl.BlockSpec((rb, n), lambda i, s: (jnp.minimum(chunk(i, s), p_chunks - 1), 0)),
                   pl.BlockSpec((rb, n), lambda i, s: (jnp.maximum(chunk(i, s) - p_chunks, 0), 0))]
    if split == "n":
        n_steps = nch
        tn = n // n_steps
        in_specs = []
        for a, pair in zip(a_parts, a_pairs):
            if pair:
                kw = a[0].shape[1]
                in_specs += [pl.BlockSpec((tm, kw), lambda i, s: (jnp.minimum(tile(i), p_tiles - 1), 0)),
                             pl.BlockSpec((tm, kw), lambda i, s: (jnp.maximum(tile(i) - p_tiles, 0), 0))]
            else:
                in_specs.append(pl.BlockSpec((tm, a.shape[1]), lambda i, s: (tile(i), 0)))
        in_specs.append(pl.BlockSpec((None, kdim, tn), lambda i, s: (layer, 0, s)))
    else:
        (a,) = a_parts
        assert not a_pairs[0]
        tn, ka = n, a.shape[1]
        tk = max(t for t in range(MXU_DEPTH, ka // nch + 1, MXU_DEPTH) if ka % t == 0)
        n_steps = ka // tk
        if ka > kdim:
            last_rows = kdim - (n_steps - 1) * tk
            assert 0 < last_rows < tk and n_steps > nch
        in_specs = [pl.BlockSpec((tm, tk), lambda i, s: (tile(i), s)),
                    pl.BlockSpec((None, tk, n), lambda i, s: (layer, s, 0))]
    in_specs += (group_specs if len(xs) == 2 else [chunk_spec(n)]) + [const((1, n))]
    args = [v for a, pair in zip(a_parts, a_pairs) for v in (a if pair else [a])] + [b] + xs + [g]
    if second == "norm":
        in_specs.append(const((1, n)))
        args.append(g2)
    if ple:
        assert split == "n"
        in_specs += [pl.BlockSpec((tm, pe.shape[1]), lambda i, s: (tile(i), 0)),
                     pl.BlockSpec((None, wple.shape[1], tn), lambda i, s: (layer, 0, s))]
        args += [pe, wple]
    if gates:
        in_specs += [const(wif.shape), const(bif.shape)]
        args += [wif, bif]
    if split_out:
        out_specs = list(group_specs)
        out_shape = [jax.ShapeDtypeStruct((split_out, n), F32), jax.ShapeDtypeStruct((m - split_out, n), F32)]
    else:
        out_specs, out_shape = [chunk_spec(n)], [jax.ShapeDtypeStruct((m, n), F32)]
    if second:
        out_specs.append(chunk_spec(n))
        out_shape.append(jax.ShapeDtypeStruct((m, n), BF16))
    if gates:
        out_specs.append(chunk_spec(GATE_PAD))
        out_shape.append(jax.ShapeDtypeStruct((m, GATE_PAD), F32))
    return pl.pallas_call(
        functools.partial(_mm_res_kernel, a_pairs=tuple(a_pairs), split=split, n_tiles=n_tiles, n_steps=n_steps,
                          nch=nch, tn=tn, rb=rb, ple=ple, second=second, gates=gates, n_heads=n_heads,
                          n_x=len(xs), n_xo=2 if split_out else 1, p_chunks=p_chunks, p_tiles=p_tiles,
                          last_rows=last_rows),
        grid=(n_tiles + 1, n_steps),
        in_specs=in_specs, out_specs=out_specs, out_shape=out_shape,
        scratch_shapes=[pltpu.VMEM((tm, n), F32), pltpu.VMEM((tm, n), F32)],
        compiler_params=pltpu.CompilerParams(
            dimension_semantics=("arbitrary", "arbitrary"),
            vmem_limit_bytes=VMEM_LIMIT if chunks == EPI_CHUNKS else VMEM_LIMIT_WIDE), name=name,
    )(*args)


def _split3(x):
    hi = x.astype(BF16)
    r1 = x - hi.astype(F32)
    mid = r1.astype(BF16)
    lo = (r1 - mid.astype(F32)).astype(BF16)
    return hi, mid, lo


def _mlstm_kernel(zq_ref, zk_ref, zv_ref, zo_ref, gate_ref, c0_ref, n0_ref, m0_ref,
                  gmh_ref, y_ref, c_ref, n_ref, m_ref, *, n_heads, dh, t):
    @pl.when(pl.program_id(1) == 0)
    def _():
        c_ref[...] = c0_ref[...]
        n_ref[...] = n0_ref[...]
        m_ref[...] = m0_ref[...]

    row = lax.broadcasted_iota(jnp.int32, (t, t), 0)
    col = lax.broadcasted_iota(jnp.int32, (t, t), 1)
    causal = row >= col
    tri = causal.astype(BF16)
    tri_t = (col >= row).astype(BF16)
    g_col = gate_ref[...]
    g_row = g_col.T
    cum_col = sum(_dot(tri, p) for p in _split3(g_col))
    cum_row = sum(_dot(p, tri_t) for p in _split3(g_row))
    scale = dh ** -0.5

    for h in range(n_heads):
        sl = slice(h * dh, (h + 1) * dh)
        q = zq_ref[:, sl]
        k = zk_ref[:, sl]
        v = zv_ref[:, sl]
        i_col = g_col[:, h:h + 1]
        i_row = g_row[h:h + 1, :]
        b_col = cum_col[:, n_heads + h:n_heads + h + 1]
        b_row = cum_row[n_heads + h:n_heads + h + 1, :]
        m_prev = m_ref[0, :, h:h + 1]
        c_prev = c_ref[0, h]
        n_prev = n_ref[0, h:h + 1, :]

        d = jnp.where(causal, b_col - b_row + i_row, -jnp.inf)
        inter = b_col + m_prev
        mt = jnp.maximum(inter, jnp.max(d, axis=1, keepdims=True))
        s = lax.dot_general(q, k, (((1,), (1,)), ((), ())), preferred_element_type=F32)
        s = s * scale * jnp.exp(d - mt)
        sc_t = jnp.exp(inter - mt)
        num = sc_t * _dot(q, c_prev.astype(BF16)) + _dot(s.astype(BF16), v)
        qn = jnp.sum(q.astype(F32) * n_prev, axis=1, keepdims=True)
        den = sc_t * qn + jnp.sum(s, axis=1, keepdims=True)
        hh = num / jnp.maximum(jnp.abs(den), jnp.exp(-mt))

        b_last = b_col[t - 1:t, :]
        m_new = mt[t - 1:t, :]
        wl = jnp.exp(b_last - b_col + i_col - m_new)
        sc = jnp.exp(b_last + m_prev - m_new)
        kw = k.astype(F32) * wl
        kv = lax.dot_general(kw.astype(BF16), v, (((0,), (0,)), ((), ())), preferred_element_type=F32)
        c_ref[0, h] = sc * c_prev + kv * scale
        n_ref[0, h:h + 1, :] = sc * n_prev + jnp.sum(kw, axis=0, keepdims=True) * scale
        m_ref[0, :, h:h + 1] = m_new

        hn = _rms(hh, gmh_ref[:, sl])
        y_ref[:, sl] = (hn * jax.nn.sigmoid(zo_ref[:, sl].astype(F32))).astype(BF16)


def mlstm(z, gates, c0, n0, m0, gmh, layer, row_off, bsz, seq, n_heads, dh):
    d_a = n_heads * dh
    t = min(256, seq)
    nc = seq // t
    r0 = row_off // t
    rmap = lambda b, c: r0 + b * nc + c
    zspec = lambda j: pl.BlockSpec((t, d_a), lambda b, c, j=j: (rmap(b, c), j))
    return pl.pallas_call(
        functools.partial(_mlstm_kernel, n_heads=n_heads, dh=dh, t=t),
        grid=(bsz, nc),
        in_specs=[zspec(0), zspec(1), zspec(2), zspec(3),
                  pl.BlockSpec((t, GATE_PAD), lambda b, c: (rmap(b, c), 0)),
                  pl.BlockSpec((None, 1, n_heads, dh, dh), lambda b, c: (layer, b, 0, 0, 0)),
                  pl.BlockSpec((1, n_heads, dh), lambda b, c: (b, 0, 0)),
                  pl.BlockSpec((1, 1, LANES), lambda b, c: (b, 0, 0)),
                  pl.BlockSpec((1, d_a), lambda b, c: (0, 0))],
        out_specs=[pl.BlockSpec((t, d_a), lambda b, c: (b * nc + c, 0)),
                   pl.BlockSpec((1, n_heads, dh, dh), lambda b, c: (b, 0, 0, 0)),
                   pl.BlockSpec((1, n_heads, dh), lambda b, c: (b, 0, 0)),
                   pl.BlockSpec((1, 1, LANES), lambda b, c: (b, 0, 0))],
        out_shape=[jax.ShapeDtypeStruct((bsz * seq, d_a), BF16),
                   jax.ShapeDtypeStruct((bsz, n_heads, dh, dh), F32),
                   jax.ShapeDtypeStruct((bsz, n_heads, dh), F32),
                   jax.ShapeDtypeStruct((bsz, 1, LANES), F32)],
        compiler_params=_params("parallel", "arbitrary"), name="mlstm",
    )(z, z, z, z, gates, c0, n0, m0, gmh)


S5_ROWS = 512


def _s5_disc_kernel(lr_ref, li_ref, ldt_ref, br_ref, bi_ref, abr_ref, abi_ref, bbr_ref, bbi_ref):
    dt = jnp.exp(ldt_ref[...])
    lr = lr_ref[...]
    li = li_ref[...]
    mag = jnp.exp(lr * dt)
    ab_re = mag * jnp.cos(li * dt)
    ab_im = mag * jnp.sin(li * dt)
    den = lr * lr + li * li
    nr = ab_re - 1.0
    coef_re = (nr * lr + ab_im * li) / den
    coef_im = (ab_im * lr - nr * li) / den
    br = br_ref[...]
    bi = bi_ref[...]
    abr_ref[...] = ab_re
    abi_ref[...] = ab_im
    bbr_ref[...] = coef_re * br - coef_im * bi
    bbi_ref[...] = coef_re * bi + coef_im * br


def s5_discretise(lam_re, lam_im, log_dt, b_re, b_im):
    g, p, j = b_re.shape
    rep = lambda a: jnp.repeat(a, j, axis=0)
    br_t = jnp.swapaxes(b_re, 1, 2).reshape(g * j, p)
    bi_t = jnp.swapaxes(b_im, 1, 2).reshape(g * j, p)
    shp = jax.ShapeDtypeStruct((g * j, p), F32)
    abr, abi, bbr, bbi = pl.pallas_call(
        _s5_disc_kernel, out_shape=[shp, shp, shp, shp], name="s5_disc",
    )(rep(lam_re), rep(lam_im), rep(log_dt[:, None]), br_t, bi_t)
    first = lambda a: a.reshape(g, j, p)[:, 0]
    return first(abr), first(abi), bbr.reshape(g, j, p), bbi.reshape(g, j, p)


def _s5_scan_kernel(*refs, s_rows, tc, nq, gw, slab, emit_y):
    if emit_y:
        u_ref, h0_ref, a_ref, bblk_ref, cblk_ref, d_ref, wglu_ref, y_ref, hout_ref, bu_scr = refs
    else:
        u_ref, h0_ref, a_ref, bblk_ref, hout_ref, bu_scr = refs

    @pl.when(pl.program_id(0) == 0)
    def _():
        hout_ref[...] = h0_ref[...]

    u = u_ref[...]
    kq = u.shape[1] // nq
    for q in range(nq):
        bu_scr[:, q * 2 * gw:(q + 1) * 2 * gw] = _dot(u[:, q * kq:(q + 1) * kq], bblk_ref[q])

    for q in range(nq):
        for lo in range(0, gw, slab):
            re = slice(q * 2 * gw + lo, q * 2 * gw + lo + slab)
            im = slice(q * 2 * gw + gw + lo, q * 2 * gw + gw + lo + slab)
            ar = a_ref[:, re]
            ai = a_ref[:, im]

            def body(step, carry, re=re, im=im, ar=ar, ai=ai):
                hr, hi = carry
                rows = pl.ds(pl.multiple_of(step * s_rows, s_rows), s_rows)
                nr = ar * hr - ai * hi + bu_scr[rows, re]
                ni = ar * hi + ai * hr + bu_scr[rows, im]
                if emit_y:
                    bu_scr[rows, re] = nr
                    bu_scr[rows, im] = ni
                return nr, ni

            hr, hi = lax.fori_loop(0, tc, body, (hout_ref[:, re], hout_ref[:, im]), unroll=2)
            hout_ref[:, re] = hr
            hout_ref[:, im] = hi

    if emit_y:
        ys = [_dot(bu_scr[:, q * 2 * gw:(q + 1) * 2 * gw].astype(BF16), cblk_ref[q]) for q in range(nq)]
        y = jnp.concatenate(ys, axis=1) + d_ref[...] * u.astype(F32)
        zg = 0.5 * y * (1.0 + jnp.tanh(math.sqrt(2.0 / math.pi) * (y + 0.044715 * (y * y * y))))
        y_ref[...] = (zg * jax.nn.sigmoid(_dot(zg.astype(BF16), wglu_ref[...]))).astype(BF16)


def s5_scan(u_perm, h0, a_blk, bblk, cblk, d, wglu, s_rows, emit_y):
    rows, d_b = u_perm.shape
    nq, kq, w2 = bblk.shape
    gw = w2 // 2
    steps = rows // s_rows
    tc = _pick(steps, max(1, S5_ROWS // s_rows))
    rb = tc * s_rows
    wtot = nq * w2
    const2 = lambda c: (0, 0)
    in_specs = [pl.BlockSpec((rb, d_b), lambda c: (c, 0)),
                pl.BlockSpec((s_rows, wtot), const2),
                pl.BlockSpec((1, wtot), const2),
                pl.BlockSpec((nq, kq, w2), lambda c: (0, 0, 0))]
    args = [u_perm, h0, a_blk, bblk]
    out_specs = [pl.BlockSpec((s_rows, wtot), const2)]
    out_shape = [jax.ShapeDtypeStruct((s_rows, wtot), F32)]
    if emit_y:
        in_specs += [pl.BlockSpec((nq, w2, kq), lambda c: (0, 0, 0)),
                     pl.BlockSpec((1, d_b), const2),
                     pl.BlockSpec((d_b, d_b), const2)]
        args += [cblk, d, wglu]
        out_specs = [pl.BlockSpec((rb, d_b), lambda c: (c, 0))] + out_specs
        out_shape = [jax.ShapeDtypeStruct((rows, d_b), BF16)] + out_shape
    return pl.pallas_call(
        functools.partial(_s5_scan_kernel, s_rows=s_rows, tc=tc, nq=nq, gw=gw,
                          slab=min(512, gw), emit_y=emit_y),
        grid=(steps // tc,),
        in_specs=in_specs, out_specs=out_specs, out_shape=out_shape,
        scratch_shapes=[pltpu.VMEM((rb, wtot), F32)],
        compiler_params=_params("arbitrary"), name="s5_scan" if emit_y else "s5_ends",
    )(*args)


def _s5_chain_kernel(sfin_ref, a_ref, hin_ref, *, seg_len, n_seg, nq, gw):
    for q in range(nq):
        re = slice(q * 2 * gw, q * 2 * gw + gw)
        im = slice(q * 2 * gw + gw, (q + 1) * 2 * gw)
        br, bi = a_ref[:, re], a_ref[:, im]
        pr, pi = jnp.ones_like(br), jnp.zeros_like(br)
        e = seg_len
        while e:
            if e & 1:
                pr, pi = pr * br - pi * bi, pr * bi + pi * br
            br, bi = br * br - bi * bi, 2.0 * br * bi
            e >>= 1
        hr, hi = jnp.zeros_like(pr), jnp.zeros_like(pr)
        for s in range(n_seg):
            hin_ref[s:s + 1, re] = hr
            hin_ref[s:s + 1, im] = hi
            hr, hi = (pr * hr - pi * hi + sfin_ref[s:s + 1, re],
                      pr * hi + pi * hr + sfin_ref[s:s + 1, im])


def s5_chain(sfin, a_blk, seg_len, nq):
    n_seg, wtot = sfin.shape
    return pl.pallas_call(
        functools.partial(_s5_chain_kernel, seg_len=seg_len, n_seg=n_seg, nq=nq, gw=wtot // nq // 2),
        out_shape=jax.ShapeDtypeStruct((n_seg, wtot), F32), name="s5_chain",
    )(sfin, a_blk)


def _conv_kernel(val_ref, gate_ref, cache_ref, w_ref, b_ref, lng_ref, lnb_ref, y_ref, cout_ref, xp_scr, xs_scr,
                 *, t, width):
    c = pl.program_id(1)

    @pl.when(c == 0)
    def _():
        xp_scr[0:CACHE_PAD, :] = cache_ref[0]

    @pl.when(c > 0)
    def _():
        xp_scr[0:CACHE_PAD, :] = xp_scr[t:t + CACHE_PAD, :]

    xp_scr[CACHE_PAD:CACHE_PAD + t, :] = val_ref[...].astype(F32) * jax.nn.sigmoid(gate_ref[...].astype(F32))
    first = CACHE_PAD - (width - 1)
    acc = None
    for r in range(min(SUBLANES, width)):
        taps = range(r, width, SUBLANES)
        rows = t + SUBLANES * (len(taps) - 1)
        xs_scr[0:rows, :] = xp_scr[first + r:first + r + rows, :]
        for a, j in enumerate(taps):
            term = w_ref[j:j + 1, :] * xs_scr[SUBLANES * a:SUBLANES * a + t, :]
            acc = term if acc is None else acc + term
    y = acc + b_ref[...]
    mu = jnp.mean(y, axis=-1, keepdims=True)
    yc = y - mu
    var = jnp.mean(yc * yc, axis=-1, keepdims=True)
    y = yc * lax.rsqrt(var + EPS) * lng_ref[...] + lnb_ref[...]
    y_ref[...] = (y * jax.nn.sigmoid(y)).astype(BF16)
    cout_ref[0] = xp_scr[t:t + CACHE_PAD, :]


def conv_module(z, cache, w, b, lng, lnb, row_off, bsz, seq, col_val, col_gate):
    width, d_c = w.shape
    t = min(256, seq)
    assert t >= CACHE_PAD
    nc = seq // t
    r0 = row_off // t
    rmap = lambda bb, c: r0 + bb * nc + c
    vec = pl.BlockSpec((1, d_c), lambda bb, c: (0, 0))
    return pl.pallas_call(
        functools.partial(_conv_kernel, t=t, width=width),
        grid=(bsz, nc),
        in_specs=[pl.BlockSpec((t, d_c), lambda bb, c: (rmap(bb, c), col_val)),
                  pl.BlockSpec((t, d_c), lambda bb, c: (rmap(bb, c), col_gate)),
                  pl.BlockSpec((1, CACHE_PAD, d_c), lambda bb, c: (bb, 0, 0)),
                  pl.BlockSpec((width, d_c), lambda bb, c: (0, 0)),
                  vec, vec, vec],
        out_specs=[pl.BlockSpec((t, d_c), lambda bb, c: (bb * nc + c, 0)),
                   pl.BlockSpec((1, CACHE_PAD, d_c), lambda bb, c: (bb, 0, 0))],
        out_shape=[jax.ShapeDtypeStruct((bsz * seq, d_c), BF16),
                   jax.ShapeDtypeStruct((bsz, CACHE_PAD, d_c), F32)],
        scratch_shapes=[pltpu.VMEM((CACHE_PAD + t, d_c), F32), pltpu.VMEM((CACHE_PAD + t, d_c), F32)],
        compiler_params=_params("parallel", "arbitrary"), name="conv",
    )(z, z, cache, w, b, lng, lnb)


def _block_diag(w, gq):
    g, r, c = w.shape
    eye = jnp.eye(gq, dtype=w.dtype)
    w = w.reshape(g // gq, gq, r, c)
    return jnp.einsum("qgrc,gh->qgrhc", w, eye).reshape(g // gq, gq * r, gq * c)


def _state_to_lanes(re, im, gq):
    s, g, p = re.shape
    both = jnp.stack([re.reshape(s, g // gq, gq * p), im.reshape(s, g // gq, gq * p)], axis=2)
    return both.reshape(s, -1)


def _lanes_to_state(h, g, p, gq):
    s = h.shape[0]
    both = h.reshape(s, g // gq, 2, gq * p)
    return both[:, :, 0].reshape(s, g, p), both[:, :, 1].reshape(s, g, p)


def _to_step_major(x, n_seq):
    rows, d = x.shape
    return x.reshape(n_seq, rows // n_seq, d).swapaxes(0, 1).reshape(rows, d)


def _from_step_major(x, n_seq):
    rows, d = x.shape
    return x.reshape(rows // n_seq, n_seq, d).swapaxes(0, 1).reshape(rows, d)


PROMPT_SEGMENTS = 32


def kernel(x_prompt, x_sample, p_prompt, p_sample, state_mlstm_c, state_mlstm_n, state_mlstm_m, state_s5_re, state_s5_im, cache_conv, g_pre_mix, w_in, b_igate, b_fgate, g_mh, s5_lam_re, s5_lam_im, s5_log_dt, s5_b_re, s5_b_im, s5_c_re, s5_c_im, s5_d, s5_w_glu, conv_w, conv_b, conv_ln_g, conv_ln_b, w_out, g_post_mix, g_pre_ffn, w_ffn_gate, w_ffn_up, w_ffn_down, g_post_ffn, w_ple, w_ple_gate, g_post_ple):
    depth = w_in.shape[0]
    bp, lp, d_model = x_prompt.shape
    bs, ls, _ = x_sample.shape
    n_heads = b_igate.shape[1]
    d_a = g_mh.shape[1]
    dh = d_a // n_heads
    g_b, p_b, j_b = s5_b_re.shape[1:]
    d_b = g_b * j_b
    width, d_c = conv_w.shape[1:]
    d_ff = w_ffn_gate.shape[2]
    assert bp == 1 and 2 * n_heads <= GATE_PAD and d_b == d_c and d_a % d_b == 0
    gq = 256 // j_b
    nq = g_b // gq
    np_rows, ns_rows = bp * lp, bs * ls
    n_seg = min(PROMPT_SEGMENTS, lp // 8)
    seg_len = lp // n_seg
    col_u = 4 * d_a // d_b
    col_val, col_gate = col_u + 1, col_u + 2
    d_ff_pad = -(-d_ff // 1024) * 1024

    x = (x_prompt.reshape(np_rows, d_model), x_sample.reshape(ns_rows, d_model))
    row = lambda v: v.reshape(1, -1).astype(F32)
    zeros = lambda *s: jnp.zeros(s, F32)
    pad_m = lambda m: jnp.pad(m, ((0, 0), (0, LANES - n_heads)))[:, None, :]
    pad_cache = lambda c: jnp.pad(c, ((0, 0), (CACHE_PAD - (width - 1), 0), (0, 0)))

    n_qkvo = 4 * d_a
    w_ifs = [jnp.pad(w_in[i][:, n_qkvo:n_qkvo + 2 * n_heads], ((0, 0), (0, GATE_PAD - 2 * n_heads))).astype(BF16)
             for i in range(depth)]
    b_ifs = [jnp.pad(jnp.concatenate([b_igate[i], b_fgate[i]]), (0, GATE_PAD - 2 * n_heads)).reshape(1, GATE_PAD)
             for i in range(depth)]
    w_gate_all, w_up_all = w_ffn_gate.astype(BF16), w_ffn_up.astype(BF16)
    w_down_all = w_ffn_down.astype(BF16)
    w_out_all, w_pgate_all, w_ple_all = w_out.astype(BF16), w_ple_gate.astype(BF16), w_ple.astype(BF16)
    zero_c = zeros(1, bp, n_heads, dh, dh)
    outs = {k: [] for k in ("pc", "sc", "pn", "pm", "pre", "pim", "pcv", "sn", "sm", "sre", "sim", "scv")}
    for i in range(depth):
        w_main = jnp.concatenate([w_in[i][:, :n_qkvo], w_in[i][:, n_qkvo + 2 * n_heads:]], axis=1).astype(BF16)

        ab_re, ab_im, bb_re, bb_im = s5_discretise(s5_lam_re[i], s5_lam_im[i], s5_log_dt[i], s5_b_re[i], s5_b_im[i])
        bblk = jnp.concatenate([_block_diag(bb_re, gq), _block_diag(bb_im, gq)], axis=2).astype(BF16)
        c_re_t = jnp.swapaxes(s5_c_re[i], 1, 2)
        c_im_t = jnp.swapaxes(s5_c_im[i], 1, 2)
        cblk = jnp.concatenate([_block_diag(c_re_t, gq), _block_diag(-c_im_t, gq)], axis=1).astype(BF16)
        a_blk = _state_to_lanes(ab_re[None], ab_im[None], gq)
        wglu = s5_w_glu[i].astype(BF16)
        d_row = row(s5_d[i])

        if i == 0:
            h, gates = rms_gates(x[0], x[1], row(g_pre_mix[i]), w_ifs[i], b_ifs[i], n_heads)
        z = mm(h, w_main, BF16)

        ya_p, c1, n1, m1 = mlstm(z, gates, zero_c, zeros(bp, n_heads, dh), zeros(bp, 1, LANES), row(g_mh[i]),
                                 0, 0, bp, lp, n_heads, dh)
        ya_s, c2, n2, m2 = mlstm(z, gates, state_mlstm_c, state_mlstm_n[i], pad_m(state_mlstm_m[i]), row(g_mh[i]),
                                 i, np_rows, bs, ls, n_heads, dh)

        u_all = z[:, 4 * d_a:4 * d_a + d_b]
        u_p = _to_step_major(u_all[:np_rows], n_seg)
        u_s = _to_step_major(u_all[np_rows:], bs)
        wtot = a_blk.shape[1]
        (sfin,) = s5_scan(u_p, zeros(n_seg, wtot), a_blk, bblk, None, None, None, n_seg, False)
        hin = s5_chain(sfin, a_blk, seg_len, nq)
        yb_p, hfin_p = s5_scan(u_p, hin, a_blk, bblk, cblk, d_row, wglu, n_seg, True)
        yb_s, hfin_s = s5_scan(u_s, _state_to_lanes(state_s5_re[i], state_s5_im[i], gq), a_blk, bblk, cblk,
                               d_row, wglu, bs, True)
        yb_p, yb_s = _from_step_major(yb_p, n_seg), _from_step_major(yb_s, bs)
        re_p, im_p = _lanes_to_state(hfin_p[n_seg - 1:], g_b, p_b, gq)
        re_s, im_s = _lanes_to_state(hfin_s, g_b, p_b, gq)

        cw, cb, lg, lb = conv_w[i], row(conv_b[i]), row(conv_ln_g[i]), row(conv_ln_b[i])
        yc_p, cv_p = conv_module(z, zeros(bp, CACHE_PAD, d_c), cw, cb, lg, lb, 0, bp, lp, col_val, col_gate)
        yc_s, cv_s = conv_module(z, pad_cache(cache_conv[i]), cw, cb, lg, lb, np_rows, bs, ls, col_val, col_gate)

        x, h = mm_res([(ya_p, ya_s), (yb_p, yb_s), (yc_p, yc_s)], w_out_all, x, row(g_post_mix[i]), split="n",
                      layer=i, second="norm", g2=row(g_pre_ffn[i]), name="mm_out")

        f = mm_swiglu(h, w_gate_all, w_up_all, i, d_ff_pad)
        x, xb = mm_res([f], w_down_all, x, row(g_post_ffn[i]), split="k", layer=i, second="cast",
                       name="mm_down")

        pe = jnp.concatenate([p_prompt[i].reshape(np_rows, -1), p_sample[i].reshape(ns_rows, -1)], axis=0).astype(BF16)
        ple_args = dict(split="n", layer=i, pe=pe, wple=w_ple_all, chunks=EPI_CHUNKS // 2, name="mm_ple")
        if i + 1 < depth:
            x, h, gates = mm_res([xb], w_pgate_all, x, row(g_post_ple[i]), second="norm",
                                 g2=row(g_pre_mix[i + 1]), wif=w_ifs[i + 1], bif=b_ifs[i + 1], n_heads=n_heads,
                                 **ple_args)
        else:
            x = mm_res([xb], w_pgate_all, x, row(g_post_ple[i]), split_out=np_rows, **ple_args)

        first = CACHE_PAD - (width - 1)
        outs["pc"].append(c1); outs["pn"].append(n1); outs["pm"].append(m1[:, 0, :n_heads])
        outs["pre"].append(re_p); outs["pim"].append(im_p); outs["pcv"].append(cv_p[:, first:])
        outs["sc"].append(c2); outs["sn"].append(n2); outs["sm"].append(m2[:, 0, :n_heads])
        outs["sre"].append(re_s); outs["sim"].append(im_s); outs["scv"].append(cv_s[:, first:])

    st = lambda k: jnp.stack(outs[k])
    return (x[0].reshape(bp, lp, d_model), x[1].reshape(bs, ls, d_model),
            st("pc"), st("pn"), st("pm"), st("pre"), st("pim"), st("pcv"),
            st("sc"), st("sn"), st("sm"), st("sre"), st("sim"), st("scv"))
```
